```python
import math, functools
import jax, jax.numpy as jnp
from jax import lax
import numpy as np

D_MODEL = 1024
BATCH = 16
SEQ = 4096
DEPTH = 1
DEC_BATCH = 128
DEC_SEQ = 8
PAST_LEN = 8192
PAGE_SIZE = 128

F32 = jnp.float32
N_META = 16
DN_HEADS = 8
DN_DK = 64
DN_DV = 64
DN_QK = DN_HEADS * DN_DK
DN_VW = DN_HEADS * DN_DV
DN_CONV_CH = 2 * DN_QK + DN_VW
CONV_W = 4
DN_CHUNK = 64
DA_HEADS = 4
DA_DH = 64
DA_QK = DA_HEADS * 2 * DA_DH
DA_V = DA_HEADS * 2 * DA_DH
Q_BLOCK = 128
ROPE_THETA = 10000.0
RMS_EPS = 1e-6
PEER_NKEYS = 128
PEER_EXPERTS = PEER_NKEYS * PEER_NKEYS
PEER_HEADS = 8
PEER_TOPK = 16
PEER_DK = 128
PEER_DK_HALF = PEER_DK // 2
PEER_BLOCK = 256
PROJ_SIZES = (DN_QK, DN_QK, DN_VW, DN_VW, DN_HEADS, DN_HEADS, DA_QK, DA_QK, DA_V, D_MODEL, D_MODEL)
PROJ_WIDTH = 4 * DN_QK + 2 * DN_HEADS + 3 * DA_QK + 2 * D_MODEL

kernel_name = 'hybrid_gdn_diffattn_peer_step'


def _split_points():
    pts, acc = [], 0
    for s in PROJ_SIZES[:-1]:
        acc += s
        pts.append(acc)
    return pts


def _lambda_init(layer):
    return 0.8 - 0.6 * math.exp(-0.3 * layer)


def _rms(x, g, eps=RMS_EPS):
    xf = x.astype(F32)
    y = xf * lax.rsqrt(jnp.mean(xf * xf, axis=-1, keepdims=True) + eps)
    return (y * g.astype(F32)).astype(x.dtype)


def _l2n(x, eps=1e-6):
    xf = x.astype(F32)
    return xf * lax.rsqrt(jnp.sum(xf * xf, axis=-1, keepdims=True) + eps)


def _rope(x, pos):
    half = x.shape[-1] // 2
    inv = ROPE_THETA ** (-jnp.arange(half, dtype=F32) / half)
    ang = pos.astype(F32)[:, None] * inv[None, :]
    cos = jnp.cos(ang)[None, :, None, :]
    sin = jnp.sin(ang)[None, :, None, :]
    xf = x.astype(F32)
    x1, x2 = xf[..., :half], xf[..., half:]
    return jnp.concatenate([x1 * cos - x2 * sin, x2 * cos + x1 * sin], axis=-1).astype(x.dtype)


def _gated_delta(q, k, v, g, beta, s0, chunk):
    B, T, H, DK = q.shape
    DV = v.shape[-1]
    n = T // chunk

    def blocks(t):
        t = t.astype(F32).reshape((B, n, chunk, H) + t.shape[3:])
        return jnp.swapaxes(t, 2, 3)

    q, k, v, g, beta = blocks(q), blocks(k), blocks(v), blocks(g), blocks(beta)
    gc = jnp.cumsum(g, axis=-1)
    idx = jnp.arange(chunk)
    tril = idx[:, None] >= idx[None, :]
    strict = idx[:, None] > idx[None, :]
    gamma = jnp.where(tril, jnp.exp(jnp.where(tril, gc[..., :, None] - gc[..., None, :], 0.0)), 0.0)
    kb = k * beta[..., None]
    a = jnp.where(strict, jnp.einsum('bnhid,bnhjd->bnhij', kb, k) * gamma, 0.0)
    eye = jnp.eye(chunk, dtype=F32)
    t_inv = lax.linalg.triangular_solve(eye + a, jnp.broadcast_to(eye, a.shape),
                                        left_side=True, lower=True, unit_diagonal=True)
    u = jnp.einsum('bnhij,bnhje->bnhie', t_inv, v * beta[..., None])
    w = jnp.einsum('bnhij,bnhjd->bnhid', t_inv, kb * jnp.exp(gc)[..., None])
    qk = jnp.einsum('bnhid,bnhjd->bnhij', q, k) * gamma
    q_dec = q * jnp.exp(gc)[..., None]
    k_dec = k * jnp.exp(gc[..., -1:] - gc)[..., None]
    g_last = jnp.exp(gc[..., -1])

    def step(s, xs):
        u_c, w_c, qd_c, kd_c, qk_c, gl_c = xs
        v_new = u_c - jnp.einsum('bhcd,bhde->bhce', w_c, s)
        o = jnp.einsum('bhcd,bhde->bhce', qd_c, s) + jnp.einsum('bhij,bhje->bhie', qk_c, v_new)
        s = s * gl_c[..., None, None] + jnp.einsum('bhcd,bhce->bhde', kd_c, v_new)
        return s, o

    xs = tuple(jnp.moveaxis(t, 1, 0) for t in (u, w, q_dec, k_dec, qk, g_last))
    s_fin, o = lax.scan(step, s0.astype(F32), xs)
    o = jnp.transpose(o, (1, 0, 3, 2, 4)).reshape(B, T, H, DV)
    return o, s_fin


def _masked_softmax(s, mask):
    return jax.nn.softmax(jnp.where(mask, s, -jnp.inf), axis=-1)


def _diff_attn_prompt(q1, q2, k, v, lam):
    B, L, H, _ = q1.shape
    Lp = -(-L // Q_BLOCK) * Q_BLOCK
    pw = ((0, 0), (0, Lp - L), (0, 0), (0, 0))
    q1, q2, k, v = (jnp.pad(t, pw) for t in (q1, q2, k, v))
    nb = Lp // Q_BLOCK
    qb = jnp.stack([q1, q2], 0).reshape(2, B, nb, Q_BLOCK, H, DA_DH).transpose(2, 0, 1, 3, 4, 5)
    k1, k2 = k[..., :DA_DH], k[..., DA_DH:]
    kpos = jnp.arange(Lp)
    scale = DA_DH ** -0.5

    def block(args):
        qblk, i = args
        qpos = i * Q_BLOCK + jnp.arange(Q_BLOCK)
        mask = kpos[None, :] <= qpos[:, None]
        s1 = jnp.einsum('bqhd,bkhd->bhqk', qblk[0], k1).astype(F32) * scale
        s2 = jnp.einsum('bqhd,bkhd->bhqk', qblk[1], k2).astype(F32) * scale
        att = _masked_softmax(s1, mask) - lam * _masked_softmax(s2, mask)
        return jnp.einsum('bhqk,bkhd->bqhd', att.astype(v.dtype), v)

    out = lax.map(block, (qb, jnp.arange(nb)))
    out = out.transpose(1, 0, 2, 3, 4).reshape(B, Lp, H, 2 * DA_DH)
    return out[:, :L]


def _pair_update(carry, q1, q2, k, v, mask):
    new = []
    for (m, l, acc), q, kk in zip(carry, (q1, q2), (k[..., :DA_DH], k[..., DA_DH:])):
        s = jnp.einsum('bqhd,bkhd->bhqk', q, kk) * DA_DH ** -0.5
        if mask is not None:
            s = jnp.where(mask, s, -jnp.inf)
        m_new = jnp.maximum(m, jnp.max(s, axis=-1))
        p = jnp.exp(s - m_new[..., None])
        corr = jnp.exp(m - m_new)
        new.append((m_new, l * corr + jnp.sum(p, axis=-1),
                    acc * corr[..., None] + jnp.einsum('bhqk,bkhd->bhqd', p, v)))
    return tuple(new)


def _diff_attn_sample(q1, q2, k, v, lam, cache_k, cache_v, page_table, layer):
    DB, S, H, _ = q1.shape
    q1, q2 = q1.astype(F32), q2.astype(F32)
    m0 = jnp.full((DB, H, S), jnp.finfo(F32).min, F32)
    l0 = jnp.zeros((DB, H, S), F32)
    a0 = jnp.zeros((DB, H, S, 2 * DA_DH), F32)
    carry = ((m0, l0, a0), (m0, l0, a0))

    def step(c, phys):
        kp = cache_k[layer, phys].astype(F32)
        vp = cache_v[layer, phys].astype(F32)
        return _pair_update(c, q1, q2, kp, vp, None), None

    carry, _ = lax.scan(step, carry, page_table.T)
    causal = jnp.arange(S)[None, :] <= jnp.arange(S)[:, None]
    (m1, l1, acc1), (m2, l2, acc2) = _pair_update(carry, q1, q2, k.astype(F32), v.astype(F32), causal)
    o = acc1 / l1[..., None] - lam * acc2 / l2[..., None]
    return jnp.transpose(o, (0, 2, 1, 3)).astype(v.dtype)


def _peer(x, w_peer_q, peer_sub_keys, expert_u, expert_v):
    shp = x.shape
    xt = x.reshape(-1, shp[-1])
    n = xt.shape[0]
    xt = jnp.pad(xt, ((0, (-n) % PEER_BLOCK), (0, 0)))
    xb = xt.reshape(-1, PEER_BLOCK, shp[-1])

    def block(xc):
        q = (xc @ w_peer_q).reshape(PEER_BLOCK, PEER_HEADS, 2, PEER_DK_HALF)
        s = jnp.einsum('thcd,hcnd->thcn', q, peer_sub_keys).astype(F32)
        s_top, i_top = lax.top_k(s, PEER_TOPK)
        cand = s_top[:, :, 0, :, None] + s_top[:, :, 1, None, :]
        cidx = i_top[:, :, 0, :, None] * PEER_NKEYS + i_top[:, :, 1, None, :]
        best, j = lax.top_k(cand.reshape(PEER_BLOCK, PEER_HEADS, -1), PEER_TOPK)
        eidx = jnp.take_along_axis(cidx.reshape(PEER_BLOCK, PEER_HEADS, -1), j, axis=-1)
        gate = jax.nn.softmax(best, axis=-1)
        u = expert_u[eidx]
        ve = expert_v[eidx]
        act = jax.nn.gelu(jnp.einsum('td,thkd->thk', xc, u).astype(F32), approximate=False)
        return jnp.einsum('thk,thkd->td', (gate * act).astype(ve.dtype), ve)

    y = lax.map(block, xb).reshape(-1, shp[-1])[:n]
    return y.reshape(shp)


def _layer(h, pos, conv_buf, s0, delta_pad, delta_chunk, attend, lam_init,
           norm_mix, w_in, conv_w, a_log, dt_bias, dn_norm, q_norm, k_norm,
           lambda_q1, lambda_k1, lambda_q2, lambda_k2, attn_subln,
           w_branch_dn, w_branch_attn, w_out, norm_ffn,
           w_peer_q, peer_sub_keys, expert_u, expert_v):
    B, T, _ = h.shape
    n = _rms(h, norm_mix)
    proj = n @ w_in
    dq, dk, dv, dz, db, da, aq, ak, av, ga, gb = jnp.split(proj, _split_points(), axis=-1)

    conv_in = jnp.concatenate([dq, dk, dv], axis=-1)
    padded = jnp.concatenate([conv_buf.astype(conv_in.dtype), conv_in], axis=1)
    conv_out = lax.conv_general_dilated(padded, conv_w[:, None, :].astype(padded.dtype), (1,), 'VALID',
                                        dimension_numbers=('NWC', 'WIO', 'NWC'),
                                        feature_group_count=DN_CONV_CH)
    conv_out = jax.nn.silu(conv_out)
    new_conv = padded[:, -(CONV_W - 1):]
    cq, ck, cv = jnp.split(conv_out, [DN_QK, 2 * DN_QK], axis=-1)
    q = _l2n(cq.reshape(B, T, DN_HEADS, DN_DK)) * DN_DK ** -0.5
    k = _l2n(ck.reshape(B, T, DN_HEADS, DN_DK))
    v = cv.reshape(B, T, DN_HEADS, DN_DV).astype(F32)
    beta = jax.nn.sigmoid(db.astype(F32))
    g = -jnp.exp(a_log.astype(F32)) * jax.nn.softplus(da.astype(F32) + dt_bias.astype(F32))
    if delta_pad:
        pw = ((0, 0), (delta_pad, 0), (0, 0))
        q, k, v = (jnp.pad(t, pw + ((0, 0),)) for t in (q, k, v))
        g, beta = jnp.pad(g, pw), jnp.pad(beta, pw)
    o, s_new = _gated_delta(q, k, v, g, beta, s0, delta_chunk)
    o = o[:, delta_pad:].astype(h.dtype)
    o_dn = (_rms(o, dn_norm) * jax.nn.silu(dz.reshape(B, T, DN_HEADS, DN_DV))).reshape(B, T, DN_VW)

    aq = aq.reshape(B, T, DA_HEADS, 2, DA_DH)
    ak = ak.reshape(B, T, DA_HEADS, 2, DA_DH)
    q1 = _rope(_rms(aq[..., 0, :], q_norm), pos)
    q2 = _rope(_rms(aq[..., 1, :], q_norm), pos)
    k1 = _rope(_rms(ak[..., 0, :], k_norm), pos)
    k2 = _rope(_rms(ak[..., 1, :], k_norm), pos)
    k_rows = jnp.concatenate([k1, k2], axis=-1)
    v_rows = av.reshape(B, T, DA_HEADS, 2 * DA_DH)
    lam = (jnp.exp(jnp.sum(lambda_q1.astype(F32) * lambda_k1.astype(F32)))
           - jnp.exp(jnp.sum(lambda_q2.astype(F32) * lambda_k2.astype(F32))) + lam_init)
    oa = attend(q1, q2, k_rows, v_rows, lam)
    o_attn = (_rms(oa, attn_subln) * (1.0 - lam_init)).reshape(B, T, DA_V)

    merged = jax.nn.sigmoid(ga) * (o_dn @ w_branch_dn) + jax.nn.sigmoid(gb) * (o_attn @ w_branch_attn)
    h = h + merged @ w_out

    h = h + _peer(_rms(h, norm_ffn), w_peer_q, peer_sub_keys, expert_u, expert_v)
    return h, (k_rows, v_rows, new_conv, s_new.astype(h.dtype))


def setup_inputs(seed: int = 0) -> dict:
    key = jax.random.key(seed)
    ks = jax.random.split(key, 32)

    def nrm(i, shape, s):
        return jax.random.normal(ks[i], shape, F32) * s

    def gain(i, shape):
        return 1.0 + 0.02 * jax.random.normal(ks[i], shape, F32)

    n_pages = PAST_LEN // PAGE_SIZE
    n_pool = (5 * DEC_BATCH * n_pages + 3) // 4
    perm = jax.random.permutation(ks[4], n_pool)[:DEC_BATCH * n_pages]
    page_table = perm.reshape(DEC_BATCH, n_pages).astype(jnp.int32)
    a_log = jnp.log(jax.random.uniform(ks[11], (DEPTH, DN_HEADS), F32, 1.0, 16.0))
    dt = jnp.exp(jax.random.uniform(ks[12], (DEPTH, DN_HEADS), F32, math.log(1e-3), math.log(1e-1)))
    dt_bias = dt + jnp.log(-jnp.expm1(-dt))
    return {
        'x_prompt': nrm(0, (BATCH, SEQ, D_MODEL), 1.0),
        'x_sample': nrm(1, (DEC_BATCH, DEC_SEQ, D_MODEL), 1.0),
        'cache_k': nrm(2, (DEPTH, n_pool, PAGE_SIZE, DA_HEADS, 2 * DA_DH), 1.0),
        'cache_v': nrm(3, (DEPTH, n_pool, PAGE_SIZE, DA_HEADS, 2 * DA_DH), 1.0),
        'page_table': page_table,
        'state_conv': nrm(5, (DEPTH, DEC_BATCH, CONV_W - 1, DN_CONV_CH), 1.0),
        'state_delta': nrm(6, (DEPTH, DEC_BATCH, DN_HEADS, DN_DK, DN_DV), 0.1),
        'meta_tokens': nrm(7, (N_META, D_MODEL), 1.0),
        'norm_mix': gain(8, (DEPTH, D_MODEL)),
        'w_in': nrm(9, (DEPTH, D_MODEL, PROJ_WIDTH), D_MODEL ** -0.5),
        'conv_w': nrm(10, (DEPTH, CONV_W, DN_CONV_CH), CONV_W ** -0.5),
        'a_log': a_log,
        'dt_bias': dt_bias,
        'dn_norm': gain(13, (DEPTH, DN_DV)),
        'q_norm': gain(14, (DEPTH, DA_DH)),
        'k_norm': gain(15, (DEPTH, DA_DH)),
        'lambda_q1': nrm(16, (DEPTH, DA_DH), 0.1),
        'lambda_k1': nrm(17, (DEPTH, DA_DH), 0.1),
        'lambda_q2': nrm(18, (DEPTH, DA_DH), 0.1),
        'lambda_k2': nrm(19, (DEPTH, DA_DH), 0.1),
        'attn_subln': gain(20, (DEPTH, 2 * DA_DH)),
        'w_branch_dn': nrm(21, (DEPTH, DN_VW, D_MODEL), DN_VW ** -0.5),
        'w_branch_attn': nrm(22, (DEPTH, DA_V, D_MODEL), DA_V ** -0.5),
        'w_out': nrm(23, (DEPTH, D_MODEL, D_MODEL), D_MODEL ** -0.5),
        'norm_ffn': gain(24, (DEPTH, D_MODEL)),
        'w_peer_q': nrm(25, (DEPTH, D_MODEL, PEER_HEADS * PEER_DK), D_MODEL ** -0.5),
        'peer_sub_keys': nrm(26, (DEPTH, PEER_HEADS, 2, PEER_NKEYS, PEER_DK_HALF), PEER_DK_HALF ** -0.5),
        'expert_u': nrm(27, (DEPTH, PEER_EXPERTS, D_MODEL), D_MODEL ** -0.5),
        'expert_v': nrm(28, (DEPTH, PEER_EXPERTS, D_MODEL), PEER_HEADS ** -0.5),
    }


def reference(x_prompt, x_sample, cache_k, cache_v, page_table, state_conv, state_delta,
              meta_tokens, norm_mix, w_in, conv_w, a_log, dt_bias, dn_norm, q_norm, k_norm,
              lambda_q1, lambda_k1, lambda_q2, lambda_k2, attn_subln,
              w_branch_dn, w_branch_attn, w_out, norm_ffn,
              w_peer_q, peer_sub_keys, expert_u, expert_v):
    B = x_prompt.shape[0]
    dtype = x_prompt.dtype
    meta = jnp.broadcast_to(meta_tokens.astype(dtype)[None], (B, N_META, D_MODEL))
    hp = jnp.concatenate([meta, x_prompt], axis=1)
    hs = x_sample
    past_len = page_table.shape[1] * PAGE_SIZE
    pos_p = jnp.arange(hp.shape[1])
    pos_s = past_len + jnp.arange(hs.shape[1])
    delta_pad = (-N_META) % DN_CHUNK
    kp, vp, cp, sp, ks_, vs_, cs_, ss_ = [], [], [], [], [], [], [], []
    for layer in range(DEPTH):
        lw = tuple(p[layer] for p in (norm_mix, w_in, conv_w, a_log, dt_bias, dn_norm, q_norm, k_norm,
                                      lambda_q1, lambda_k1, lambda_q2, lambda_k2, attn_subln,
                                      w_branch_dn, w_branch_attn, w_out, norm_ffn,
                                      w_peer_q, peer_sub_keys, expert_u, expert_v))
        lam_init = _lambda_init(layer)
        hp, (k_r, v_r, c_r, s_r) = _layer(
            hp, pos_p, jnp.zeros((B, CONV_W - 1, DN_CONV_CH), dtype),
            jnp.zeros((B, DN_HEADS, DN_DK, DN_DV), F32), delta_pad, DN_CHUNK,
            _diff_attn_prompt, lam_init, *lw)
        kp.append(k_r); vp.append(v_r); cp.append(c_r); sp.append(s_r)
        attend_s = functools.partial(_diff_attn_sample, cache_k=cache_k, cache_v=cache_v,
                                     page_table=page_table, layer=layer)
        hs, (k_r, v_r, c_r, s_r) = _layer(
            hs, pos_s, state_conv[layer], state_delta[layer], 0, hs.shape[1],
            attend_s, lam_init, *lw)
        ks_.append(k_r); vs_.append(v_r); cs_.append(c_r); ss_.append(s_r)
    y_prompt = hp[:, N_META:]
    y_sample = hs
    return (y_prompt, y_sample, jnp.stack(kp), jnp.stack(vp), jnp.stack(cp), jnp.stack(sp),
            jnp.stack(ks_), jnp.stack(vs_), jnp.stack(cs_), jnp.stack(ss_))
```

```python
import functools
import math

import jax
import jax.numpy as jnp
from jax import lax
from jax.experimental import pallas as pl
from jax.experimental.pallas import tpu as pltpu

F32 = jnp.float32
BF16 = jnp.bfloat16
HI = lax.Precision.HIGHEST

N_META = 16
DN_HEADS = 8
DN_DK = 64
DN_QK = 512
DN_CONV_CH = 1536
CONV_W = 4
DN_CHUNK = 64
DA_HEADS = 4
DA_DH = 64
ROPE_THETA = 10000.0
RMS_EPS = 1e-6
L2_EPS = 1e-6
PEER_NKEYS = 128
PEER_HEADS = 8
PEER_TOPK = 16
PAGE_SIZE = 128
LAM_INIT = 0.8 - 0.6 * math.exp(-0.3 * 0)

LANES = 128
SUBLANES = 8
VMEM_LIMIT = 48 * 1024 * 1024

COL_CONV = 0
COL_DZ = 1536
COL_AQ = 2048
COL_AK = 2560
COL_AV = 3072
COL_GA = 3584
COL_GB = 4608
COL_DBDA = 5632
PROJ_PAD = 5760


def _nt(a, b, precision=None):
    return lax.dot_general(a, b, (((1,), (1,)), ((), ())), precision=precision,
                           preferred_element_type=F32)


def _tn(a, b, precision=None):
    return lax.dot_general(a, b, (((0,), (0,)), ((), ())), precision=precision,
                           preferred_element_type=F32)


def _mm(a, b, precision=None):
    return jnp.dot(a, b, precision=precision, preferred_element_type=F32)


def _params(sem):
    return pltpu.CompilerParams(dimension_semantics=sem, vmem_limit_bytes=VMEM_LIMIT)


def _rms_matmul_kernel(x_ref, g_ref, w_ref, o_ref, xn_ref):
    @pl.when(pl.program_id(1) == 0)
    def _():
        x = x_ref[...]
        ms = jnp.mean(x * x, axis=-1, keepdims=True)
        xn_ref[...] = (x * lax.rsqrt(ms + RMS_EPS) * g_ref[...]).astype(BF16)

    o_ref[...] = _mm(xn_ref[...], w_ref[...])


def rms_matmul(x, gain, w_bf16, tm, tn):
    n, d = x.shape
    nout = w_bf16.shape[1]
    return pl.pallas_call(
        _rms_matmul_kernel,
        grid=(n // tm, nout // tn),
        in_specs=[pl.BlockSpec((tm, d), lambda i, j: (i, 0)),
                  pl.BlockSpec((1, d), lambda i, j: (0, 0)),
                  pl.BlockSpec((d, tn), lambda i, j: (0, j))],
        out_specs=pl.BlockSpec((tm, tn), lambda i, j: (i, j)),
        out_shape=jax.ShapeDtypeStruct((n, nout), F32),
        scratch_shapes=[pltpu.VMEM((tm, d), BF16)],
        compiler_params=_params(("parallel", "arbitrary")),
        name="rms_matmul",
    )(x, gain.reshape(1, d), w_bf16)


def _softplus(x):
    return jnp.maximum(x, 0.0) + jnp.log1p(jnp.exp(-jnp.abs(x)))


def _delta_kernel(cin_ref, dz_ref, dbda_ref, cs_ref, s0_ref, cw_ref, hp_ref, gain_ref,
                  o_ref, sout_ref, ext_ref, s_ref, *, chunk, t_real, n_chunks):
    c = pl.program_id(1)
    C = chunk

    @pl.when(c == 0)
    def _():
        ext_ref[0:SUBLANES, :] = cs_ref[0]
        s_ref[...] = s0_ref[0]

    x = cin_ref[...]
    ext_ref[SUBLANES:SUBLANES + C, :] = x
    conv = x * cw_ref[3:4, :]
    for j in range(CONV_W - 1):
        off = SUBLANES - (CONV_W - 1) + j
        conv = conv + ext_ref[off:off + C, :] * cw_ref[j:j + 1, :]
    ext_ref[0:SUBLANES, :] = ext_ref[C:C + SUBLANES, :]
    conv = conv * jax.nn.sigmoid(conv)

    row = lax.broadcasted_iota(jnp.int32, (C, C), 0)
    col = lax.broadcasted_iota(jnp.int32, (C, C), 1)
    tril = row >= col
    strict = row > col
    eye = (row == col).astype(F32)

    valid = (c * C + lax.broadcasted_iota(jnp.int32, (C, LANES), 0)) < t_real
    dbda = dbda_ref[...]
    beta_all = jnp.where(valid, jax.nn.sigmoid(dbda), 0.0)
    g_all = jnp.where(valid, -jnp.exp(hp_ref[0:1, :]) * _softplus(dbda + hp_ref[1:2, :]), 0.0)
    gc_all = _mm(tril.astype(F32), g_all, HI)

    levels = int(math.log2(C)) - 1
    outs = []
    for h in range(DN_HEADS):
        sl = slice(h * DN_DK, (h + 1) * DN_DK)
        qh = conv[:, sl]
        kh = conv[:, DN_QK + h * DN_DK: DN_QK + (h + 1) * DN_DK]
        vh = conv[:, 2 * DN_QK + h * DN_DK: 2 * DN_QK + (h + 1) * DN_DK]
        qh = qh * lax.rsqrt(jnp.sum(qh * qh, axis=-1, keepdims=True) + L2_EPS) * (DN_DK ** -0.5)
        kh = kh * lax.rsqrt(jnp.sum(kh * kh, axis=-1, keepdims=True) + L2_EPS)
        bh = beta_all[:, h:h + 1]
        gh = g_all[:, DN_HEADS + h:DN_HEADS + h + 1]
        gch = gc_all[:, DN_HEADS + h:DN_HEADS + h + 1]
        gl = gc_all[C - 1:C, DN_HEADS + h:DN_HEADS + h + 1]
        gc_row = _mm(jnp.ones((C, C), F32), jnp.where(row <= col, gh, 0.0), HI)
        gamma = jnp.where(tril, jnp.exp(jnp.where(tril, gch - gc_row, 0.0)), 0.0)
        kb = kh * bh
        a = jnp.where(strict, _nt(kb, kh, HI) * gamma, 0.0)
        t_inv = eye - a
        pw = a
        for _ in range(levels):
            pw = _mm(pw, pw, HI)
            t_inv = t_inv + _mm(t_inv, pw, HI)
        egc = jnp.exp(gch)
        u = _mm(t_inv, vh * bh, HI)
        w = _mm(t_inv, kb * egc, HI)
        qk = _nt(qh, kh, HI) * gamma
        qd = qh * egc
        kd = kh * jnp.exp(gl - gch)
        s = s_ref[h]
        v_new = u - _mm(w, s, HI)
        o = _mm(qd, s, HI) + _mm(qk, v_new, HI)
        s_ref[h] = s * jnp.exp(gl) + _tn(kd, v_new, HI)
        outs.append(o * lax.rsqrt(jnp.mean(o * o, axis=-1, keepdims=True) + RMS_EPS))
    o_all = jnp.concatenate(outs, axis=-1)
    dz = dz_ref[...]
    o_ref[...] = o_all * gain_ref[...] * (dz * jax.nn.sigmoid(dz))

    @pl.when(c == n_chunks - 1)
    def _():
        sout_ref[0] = s_ref[...]


def delta_branch(proj, row_off, batch, t_pad, t_real, chunk, conv_state8, s0, conv_w8, hp, gain512):
    n_chunks = t_pad // chunk
    blk_off = row_off // chunk
    kern = functools.partial(_delta_kernel, chunk=chunk, t_real=t_real, n_chunks=n_chunks)
    rowmap = lambda col: (lambda b, c: (blk_off + b * n_chunks + c, col))
    return pl.pallas_call(
        kern,
        grid=(batch, n_chunks),
        in_specs=[pl.BlockSpec((chunk, DN_CONV_CH), rowmap(COL_CONV // DN_CONV_CH)),
                  pl.BlockSpec((chunk, DN_QK), rowmap(COL_DZ // DN_QK)),
                  pl.BlockSpec((chunk, LANES), rowmap(COL_DBDA // LANES)),
                  pl.BlockSpec((1, SUBLANES, DN_CONV_CH), lambda b, c: (b, 0, 0)),
                  pl.BlockSpec((1, DN_HEADS, DN_DK, DN_DK), lambda b, c: (b, 0, 0, 0)),
                  pl.BlockSpec((SUBLANES, DN_CONV_CH), lambda b, c: (0, 0)),
                  pl.BlockSpec((SUBLANES, LANES), lambda b, c: (0, 0)),
                  pl.BlockSpec((1, DN_QK), lambda b, c: (0, 0))],
        out_specs=[pl.BlockSpec((chunk, DN_QK), lambda b, c: (b * n_chunks + c, 0)),
                   pl.BlockSpec((1, DN_HEADS, DN_DK, DN_DK), lambda b, c: (b, 0, 0, 0))],
        out_shape=[jax.ShapeDtypeStruct((batch * t_pad, DN_QK), F32),
                   jax.ShapeDtypeStruct((batch, DN_HEADS, DN_DK, DN_DK), F32)],
        scratch_shapes=[pltpu.VMEM((chunk + SUBLANES, DN_CONV_CH), F32),
                        pltpu.VMEM((DN_HEADS, DN_DK, DN_DK), F32)],
        compiler_params=_params(("parallel", "arbitrary")),
        name=f"delta_c{chunk}",
    )(proj, proj, proj, conv_state8, s0, conv_w8, hp, gain512)


def _qk_prep_kernel(x_ref, g_ref, cos_ref, sin_ref, o_ref):
    x = x_ref[...]
    lane = lax.broadcasted_iota(jnp.int32, x.shape, 1)
    lo = lane < DA_DH
    x2 = x * x
    s_lo = jnp.sum(jnp.where(lo, x2, 0.0), axis=-1, keepdims=True)
    s_hi = jnp.sum(jnp.where(lo, 0.0, x2), axis=-1, keepdims=True)
    ms = jnp.where(lo, s_lo, s_hi) * (1.0 / DA_DH)
    y = x * lax.rsqrt(ms + RMS_EPS) * g_ref[0]
    first = (lane % DA_DH) < (DA_DH // 2)
    rot = jnp.where(first, -pltpu.roll(y, LANES - DA_DH // 2, 1), pltpu.roll(y, DA_DH // 2, 1))
    o_ref[...] = y * cos_ref[...] + rot * sin_ref[...]


def qk_prep(proj, row_off, n_rows, tm, gains8, cos, sin, tiles_per_seq):
    blk_off = row_off // tm
    tab_map = lambda i, j: (i % tiles_per_seq, 0)
    return pl.pallas_call(
        _qk_prep_kernel,
        grid=(n_rows // tm, 2 * DA_HEADS),
        in_specs=[pl.BlockSpec((tm, LANES), lambda i, j: (blk_off + i, COL_AQ // LANES + j)),
                  pl.BlockSpec((1, 1, LANES), lambda i, j: (j, 0, 0)),
                  pl.BlockSpec((tm, LANES), tab_map),
                  pl.BlockSpec((tm, LANES), tab_map)],
        out_specs=pl.BlockSpec((tm, LANES), lambda i, j: (i, j)),
        out_shape=jax.ShapeDtypeStruct((n_rows, 2 * DA_HEADS * LANES), F32),
        compiler_params=_params(("parallel", "parallel")),
        name="qk_prep",
    )(proj, gains8, cos, sin)


def _rope_tables(pos):
    half = DA_DH // 2
    inv = ROPE_THETA ** (-jnp.arange(half, dtype=F32) / half)
    ang = pos.astype(F32)[:, None] * inv[None, :]
    return jnp.tile(jnp.cos(ang), (1, LANES // half)), jnp.tile(jnp.sin(ang), (1, LANES // half))


def _lam(lp_ref):
    lp = lp_ref[...]
    s1 = jnp.sum(lp[0:1, :] * lp[1:2, :], axis=-1, keepdims=True)
    s2 = jnp.sum(lp[2:3, :] * lp[3:4, :], axis=-1, keepdims=True)
    return jnp.exp(s1) - jnp.exp(s2) + LAM_INIT


def _softmax_step(q, k, v, mask, m_prev, l_prev, acc_prev):
    s = _nt(q, k)
    if mask is not None:
        s = jnp.where(mask, s, -jnp.inf)
    m_new = jnp.maximum(m_prev, jnp.max(s, axis=-1, keepdims=True))
    p = jnp.exp(s - m_new)
    corr = jnp.exp(m_prev - m_new)
    l_new = l_prev * corr + jnp.sum(p, axis=-1, keepdims=True)
    acc_new = acc_prev * corr + _mm(p.astype(BF16), v)
    return m_new, l_new, acc_new


def _diff_out(lam, l1, a1, l2, a2, gain):
    o = a1 / l1 - lam * (a2 / l2)
    return o * lax.rsqrt(jnp.mean(o * o, axis=-1, keepdims=True) + RMS_EPS) * gain * (1.0 - LAM_INIT)


def _split_q(q):
    lane = lax.broadcasted_iota(jnp.int32, q.shape, 1)
    return (jnp.where(lane < DA_DH, q, 0.0).astype(BF16),
            jnp.where(lane < DA_DH, 0.0, q).astype(BF16))


def _flash_kernel(lp_ref, q_ref, k_ref, v_ref, gain_ref, o_ref, m_ref, l_ref, acc_ref):
    qi = pl.program_id(2)
    ki = pl.program_id(3)

    @pl.when(ki == 0)
    def _():
        m_ref[...] = jnp.full(m_ref.shape, jnp.finfo(F32).min, F32)
        l_ref[...] = jnp.zeros(l_ref.shape, F32)
        acc_ref[...] = jnp.zeros(acc_ref.shape, F32)

    def update(mask):
        qs = _split_q(q_ref[...])
        k = k_ref[...].astype(BF16)
        v = v_ref[...].astype(BF16)
        for m in range(2):
            m_new, l_new, acc_new = _softmax_step(qs[m], k, v, mask, m_ref[m], l_ref[m], acc_ref[m])
            m_ref[m] = m_new
            l_ref[m] = l_new
            acc_ref[m] = acc_new

    @pl.when(ki < qi)
    def _():
        update(None)

    @pl.when(ki == qi)
    def _():
        tq, tk = q_ref.shape[0], k_ref.shape[0]
        row = lax.broadcasted_iota(jnp.int32, (tq, tk), 0)
        col = lax.broadcasted_iota(jnp.int32, (tq, tk), 1)
        update(col <= row)
        o_ref[...] = _diff_out(_lam(lp_ref), l_ref[0], acc_ref[0], l_ref[1], acc_ref[1], gain_ref[...])


def flash_prompt(qk, proj, lam_params, subln_gain, batch, t_pad, tq):
    nq = t_pad // tq
    return pl.pallas_call(
        _flash_kernel,
        grid=(batch, DA_HEADS, nq, nq),
        in_specs=[pl.BlockSpec((SUBLANES, LANES), lambda b, h, qi, ki: (0, 0)),
                  pl.BlockSpec((tq, LANES), lambda b, h, qi, ki: (b * nq + qi, h)),
                  pl.BlockSpec((tq, LANES), lambda b, h, qi, ki: (b * nq + jnp.minimum(ki, qi), DA_HEADS + h)),
                  pl.BlockSpec((tq, LANES), lambda b, h, qi, ki: (b * nq + jnp.minimum(ki, qi), COL_AV // LANES + h)),
                  pl.BlockSpec((1, LANES), lambda b, h, qi, ki: (0, 0))],
        out_specs=pl.BlockSpec((tq, LANES), lambda b, h, qi, ki: (b * nq + qi, h)),
        out_shape=jax.ShapeDtypeStruct((batch * t_pad, DA_HEADS * LANES), F32),
        scratch_shapes=[pltpu.VMEM((2, tq, 1), F32), pltpu.VMEM((2, tq, 1), F32),
                        pltpu.VMEM((2, tq, LANES), F32)],
        compiler_params=_params(("parallel", "parallel", "parallel", "arbitrary")),
        name="flash_prompt",
    )(lam_params, qk, qk, proj, subln_gain)


def _paged_kernel(pt_ref, lp_ref, q_ref, kn_ref, vn_ref, kp_ref, vp_ref, gain_ref, o_ref,
                  m_ref, l_ref, acc_ref, *, n_pages):
    p = pl.program_id(1)

    @pl.when(p == 0)
    def _():
        m_ref[...] = jnp.full(m_ref.shape, jnp.finfo(F32).min, F32)
        l_ref[...] = jnp.zeros(l_ref.shape, F32)
        acc_ref[...] = jnp.zeros(acc_ref.shape, F32)

    def update(k_ref, v_ref, mask):
        for h in range(DA_HEADS):
            sl = slice(h * LANES, (h + 1) * LANES)
            qs = _split_q(q_ref[:, sl])
            k = k_ref[:, sl].astype(BF16)
            v = v_ref[:, sl].astype(BF16)
            for m in range(2):
                i = 2 * h + m
                m_new, l_new, acc_new = _softmax_step(qs[m], k, v, mask, m_ref[i], l_ref[i], acc_ref[i])
                m_ref[i] = m_new
                l_ref[i] = l_new
                acc_ref[i] = acc_new

    @pl.when(p < n_pages)
    def _():
        update(kp_ref, vp_ref, None)

    @pl.when(p == n_pages)
    def _():
        s = q_ref.shape[0]
        row = lax.broadcasted_iota(jnp.int32, (s, s), 0)
        col = lax.broadcasted_iota(jnp.int32, (s, s), 1)
        update(kn_ref, vn_ref, col <= row)
        lam = _lam(lp_ref)
        outs = [_diff_out(lam, l_ref[2 * h], acc_ref[2 * h], l_ref[2 * h + 1], acc_ref[2 * h + 1], gain_ref[...])
                for h in range(DA_HEADS)]
        o_ref[...] = jnp.concatenate(outs, axis=-1)


def paged_decode(page_table, qk_s, proj, row_off, cache_k, cache_v, lam_params, subln_gain, dec_batch, dec_seq):
    n_pages = page_table.shape[1]
    width = DA_HEADS * LANES
    blk_off = row_off // dec_seq
    page_map = lambda b, p, pt: (pt[b, jnp.minimum(p, n_pages - 1)], 0, 0)
    grid_spec = pltpu.PrefetchScalarGridSpec(
        num_scalar_prefetch=1,
        grid=(dec_batch, n_pages + 1),
        in_specs=[pl.BlockSpec((SUBLANES, LANES), lambda b, p, pt: (0, 0)),
                  pl.BlockSpec((dec_seq, width), lambda b, p, pt: (b, 0)),
                  pl.BlockSpec((dec_seq, width), lambda b, p, pt: (b, 1)),
                  pl.BlockSpec((dec_seq, width), lambda b, p, pt: (blk_off + b, COL_AV // width)),
                  pl.BlockSpec((None, PAGE_SIZE, width), page_map),
                  pl.BlockSpec((None, PAGE_SIZE, width), page_map),
                  pl.BlockSpec((1, LANES), lambda b, p, pt: (0, 0))],
        out_specs=pl.BlockSpec((dec_seq, width), lambda b, p, pt: (b, 0)),
        scratch_shapes=[pltpu.VMEM((2 * DA_HEADS, dec_seq, 1), F32),
                        pltpu.VMEM((2 * DA_HEADS, dec_seq, 1), F32),
                        pltpu.VMEM((2 * DA_HEADS, dec_seq, LANES), F32)])
    return pl.pallas_call(
        functools.partial(_paged_kernel, n_pages=n_pages),
        grid_spec=grid_spec,
        out_shape=jax.ShapeDtypeStruct((dec_batch * dec_seq, width), F32),
        compiler_params=_params(("parallel", "arbitrary")),
        name="paged_decode",
    )(page_table, lam_params, qk_s, qk_s, proj, cache_k, cache_v, subln_gain)


def _merge_kernel(x_ref, odn_ref, oat_ref, ga0_ref, ga1_ref, gb0_ref, gb1_ref,
                  wdn_ref, wat_ref, wout_ref, o_ref):
    half = ga0_ref.shape[1]
    bd = _mm(odn_ref[...].astype(BF16), wdn_ref[...])
    ba = _mm(oat_ref[...].astype(BF16), wat_ref[...])
    m0 = jax.nn.sigmoid(ga0_ref[...]) * bd[:, :half] + jax.nn.sigmoid(gb0_ref[...]) * ba[:, :half]
    m1 = jax.nn.sigmoid(ga1_ref[...]) * bd[:, half:] + jax.nn.sigmoid(gb1_ref[...]) * ba[:, half:]
    merged = jnp.concatenate([m0, m1], axis=-1).astype(BF16)
    o_ref[...] = x_ref[...] + _mm(merged, wout_ref[...])


def merge_out(x, o_dn, o_attn, proj, w_dn, w_at, w_out, tm):
    n, d = x.shape
    half = d // 2
    col = lambda c: (lambda i: (i, c))
    const = lambda i: (0, 0)
    return pl.pallas_call(
        _merge_kernel,
        grid=(n // tm,),
        in_specs=[pl.BlockSpec((tm, d), col(0)),
                  pl.BlockSpec((tm, o_dn.shape[1]), col(0)),
                  pl.BlockSpec((tm, o_attn.shape[1]), col(0)),
                  pl.BlockSpec((tm, half), col(COL_GA // half)),
                  pl.BlockSpec((tm, half), col(COL_GA // half + 1)),
                  pl.BlockSpec((tm, half), col(COL_GB // half)),
                  pl.BlockSpec((tm, half), col(COL_GB // half + 1)),
                  pl.BlockSpec(w_dn.shape, const),
                  pl.BlockSpec(w_at.shape, const),
                  pl.BlockSpec(w_out.shape, const)],
        out_specs=pl.BlockSpec((tm, d), col(0)),
        out_shape=jax.ShapeDtypeStruct((n, d), F32),
        compiler_params=_params(("parallel",)),
        name="merge_out",
    )(x, o_dn, o_attn, proj, proj, proj, proj, w_dn, w_at, w_out)


def _take_top(s, pos, k, payload=None):
    vals, picks = [], []
    sentinel = float(s.shape[0])
    for _ in range(k):
        m = jnp.max(s, axis=0, keepdims=True)
        first = jnp.min(jnp.where(s == m, pos, sentinel), axis=0, keepdims=True)
        hit = pos == first
        vals.append(m)
        if payload is None:
            picks.append(first)
        else:
            picks.append(jnp.max(jnp.where(hit, payload, -1.0), axis=0, keepdims=True))
        s = jnp.where(hit, -jnp.inf, s)
    return jnp.concatenate(vals, axis=0), jnp.concatenate(picks, axis=0)


def _peer_topk_kernel(q_ref, keys_ref, eidx_ref, gate_ref):
    tt = q_ref.shape[0]
    kpos = lax.broadcasted_iota(jnp.int32, (PEER_NKEYS, tt), 0).astype(F32)
    cpos = lax.broadcasted_iota(jnp.int32, (PEER_TOPK * PEER_TOPK, tt), 0).astype(F32)
    for h in range(PEER_HEADS):
        qh = q_ref[:, h * LANES:(h + 1) * LANES].astype(BF16)
        tops = []
        for c in range(2):
            s = _nt(keys_ref[2 * h + c], qh)
            tops.append(_take_top(s, kpos, PEER_TOPK))
        (v0, i0), (v1, i1) = tops
        cand = jnp.concatenate([v0[a:a + 1, :] + v1 for a in range(PEER_TOPK)], axis=0)
        cidx = jnp.concatenate([i0[a:a + 1, :] * float(PEER_NKEYS) + i1 for a in range(PEER_TOPK)], axis=0)
        best, eidx = _take_top(cand, cpos, PEER_TOPK, payload=cidx)
        e = jnp.exp(best - best[0:1, :])
        gate = e / jnp.sum(e, axis=0, keepdims=True)
        eidx_ref[h * PEER_TOPK:(h + 1) * PEER_TOPK, :] = eidx.astype(jnp.int32)
        gate_ref[h * PEER_TOPK:(h + 1) * PEER_TOPK, :] = gate


def peer_topk(qp, keys2, tt):
    n = qp.shape[0]
    rows = PEER_HEADS * PEER_TOPK
    return pl.pallas_call(
        _peer_topk_kernel,
        grid=(n // tt,),
        in_specs=[pl.BlockSpec((tt, qp.shape[1]), lambda i: (i, 0)),
                  pl.BlockSpec(keys2.shape, lambda i: (0, 0, 0))],
        out_specs=[pl.BlockSpec((rows, tt), lambda i: (0, i)),
                   pl.BlockSpec((rows, tt), lambda i: (0, i))],
        out_shape=[jax.ShapeDtypeStruct((rows, n), jnp.int32),
                   jax.ShapeDtypeStruct((rows, n), F32)],
        compiler_params=_params(("parallel",)),
        name="peer_topk",
    )(qp, keys2)


def _gelu(x):
    return 0.5 * x * (1.0 + lax.erf(x * (2.0 ** -0.5)))


def _peer_mix_kernel(idx_cur_ref, idx_nxt_ref, gate_ref, h_ref, g_ref, uv_ref, o_ref, buf_ref, sem_ref,
                     *, tt, n_sel, d, n_tiles):
    i = pl.program_id(0)
    rows = tt * n_sel
    slot = i % 2

    def issue(idx_ref, dst_slot):
        def body(r, carry):
            e = idx_ref[r // n_sel, r % n_sel]
            pltpu.make_async_copy(uv_ref.at[pl.ds(e, 1)], buf_ref.at[dst_slot, pl.ds(r, 1)],
                                  sem_ref.at[dst_slot]).start()
            return carry
        lax.fori_loop(0, rows, body, 0, unroll=8)

    @pl.when(i == 0)
    def _():
        issue(idx_cur_ref, 0)

    @pl.when(i + 1 < n_tiles)
    def _():
        issue(idx_nxt_ref, 1 - slot)

    pltpu.make_async_copy(uv_ref.at[pl.ds(0, rows)], buf_ref.at[slot], sem_ref.at[slot]).wait()

    hb = h_ref[...]
    xn = hb * lax.rsqrt(jnp.mean(hb * hb, axis=-1, keepdims=True) + RMS_EPS) * g_ref[...]
    er = lax.broadcasted_iota(jnp.int32, (n_sel, n_sel), 0)
    ec = lax.broadcasted_iota(jnp.int32, (n_sel, n_sel), 1)
    gate_t = _nt((er == ec).astype(F32), gate_ref[...], HI)
    cols = []
    for t in range(tt):
        u = buf_ref[slot, t * n_sel:(t + 1) * n_sel, 0:d]
        cols.append(jnp.sum(u * xn[t:t + 1, :], axis=-1, keepdims=True))
    w = gate_t * _gelu(jnp.concatenate(cols, axis=1))
    ys = []
    for t in range(tt):
        v = buf_ref[slot, t * n_sel:(t + 1) * n_sel, d:2 * d]
        ys.append(jnp.sum(v * w[:, t:t + 1], axis=0, keepdims=True))
    o_ref[...] = hb + jnp.concatenate(ys, axis=0)


def peer_mix(eidx, gate, h, gain, uv, tt):
    n, d = h.shape
    n_sel = eidx.shape[1]
    n_tiles = n // tt
    kern = functools.partial(_peer_mix_kernel, tt=tt, n_sel=n_sel, d=d, n_tiles=n_tiles)
    return pl.pallas_call(
        kern,
        grid=(n_tiles,),
        in_specs=[pl.BlockSpec((tt, n_sel), lambda i: (i, 0), memory_space=pltpu.SMEM),
                  pl.BlockSpec((tt, n_sel), lambda i: (jnp.minimum(i + 1, n_tiles - 1), 0),
                               memory_space=pltpu.SMEM),
                  pl.BlockSpec((tt, n_sel), lambda i: (i, 0)),
                  pl.BlockSpec((tt, d), lambda i: (i, 0)),
                  pl.BlockSpec((1, d), lambda i: (0, 0)),
                  pl.BlockSpec(memory_space=pl.ANY)],
        out_specs=pl.BlockSpec((tt, d), lambda i: (i, 0)),
        out_shape=jax.ShapeDtypeStruct((n, d), F32),
        scratch_shapes=[pltpu.VMEM((2, tt * n_sel, 2 * d), F32), pltpu.SemaphoreType.DMA((2,))],
        compiler_params=_params(("arbitrary",)),
        name="peer_mix",
    )(eidx, eidx, gate, h, gain.reshape(1, d), uv)


def _pad_lanes(v, width=LANES):
    return jnp.pad(v, (0, width - v.shape[0]))


def kernel(x_prompt, x_sample, cache_k, cache_v, page_table, state_conv, state_delta, meta_tokens, norm_mix, w_in, conv_w, a_log, dt_bias, dn_norm, q_norm, k_norm, lambda_q1, lambda_k1, lambda_q2, lambda_k2, attn_subln, w_branch_dn, w_branch_attn, w_out, norm_ffn, w_peer_q, peer_sub_keys, expert_u, expert_v):
    batch, seq, d = x_prompt.shape
    dec_batch, dec_seq, _ = x_sample.shape
    layer = 0
    t_real = N_META + seq
    tq = 3 * LANES
    t_pad = -(-t_real // tq) * tq
    np_rows = batch * t_pad
    ns_rows = dec_batch * dec_seq
    n = np_rows + ns_rows

    meta = jnp.broadcast_to(meta_tokens[None], (batch, N_META, d))
    hp = jnp.concatenate([meta, x_prompt, jnp.zeros((batch, t_pad - t_real, d), F32)], axis=1)
    x_all = jnp.concatenate([hp.reshape(np_rows, d), x_sample.reshape(ns_rows, d)], axis=0)

    wi = w_in[layer]
    pts = [0, 512, 1024, 1536, 2048, 2056, 2064, 2576, 3088, 3600, 4624, 5648]
    seg = [wi[:, pts[i]:pts[i + 1]] for i in range(11)]
    dq, dk, dv, dz, db, da, aq, ak, av, ga, gb = seg
    w_in_r = jnp.concatenate([dq, dk, dv, dz, aq, ak, av, ga, gb, db, da,
                              jnp.zeros((d, PROJ_PAD - 5648), F32)], axis=1).astype(BF16)

    proj = rms_matmul(x_all, norm_mix[layer], w_in_r, 1024, 640)

    conv_w8 = jnp.pad(conv_w[layer], ((0, SUBLANES - CONV_W), (0, 0)))
    hp_rows = jnp.zeros((SUBLANES, LANES), F32)
    hp_rows = hp_rows.at[0, DN_HEADS:2 * DN_HEADS].set(a_log[layer])
    hp_rows = hp_rows.at[1, DN_HEADS:2 * DN_HEADS].set(dt_bias[layer])
    gain512 = jnp.tile(dn_norm[layer], DN_HEADS).reshape(1, DN_QK)
    cs_p = jnp.zeros((batch, SUBLANES, DN_CONV_CH), F32)
    s0_p = jnp.zeros((batch, DN_HEADS, DN_DK, DN_DK), F32)
    o_dn_p, s_p = delta_branch(proj, 0, batch, t_pad, t_real, DN_CHUNK, cs_p, s0_p, conv_w8, hp_rows, gain512)
    cs_s = jnp.pad(state_conv[layer], ((0, 0), (SUBLANES - (CONV_W - 1), 0), (0, 0)))
    o_dn_s, s_s = delta_branch(proj, np_rows, dec_batch, dec_seq, dec_seq, dec_seq, cs_s,
                               state_delta[layer], conv_w8, hp_rows, gain512)

    gq = jnp.tile(q_norm[layer], 2) * (DA_DH ** -0.5)
    gk = jnp.tile(k_norm[layer], 2)
    gains8 = jnp.concatenate([jnp.tile(gq[None], (DA_HEADS, 1)), jnp.tile(gk[None], (DA_HEADS, 1))], axis=0)
    gains8 = gains8.reshape(2 * DA_HEADS, 1, LANES)
    cos_p, sin_p = _rope_tables(jnp.arange(t_pad))
    past_len = page_table.shape[1] * PAGE_SIZE
    cos_s, sin_s = _rope_tables(past_len + (jnp.arange(ns_rows) % dec_seq))
    qk_p = qk_prep(proj, 0, np_rows, tq, gains8, cos_p, sin_p, t_pad // tq)
    qk_s = qk_prep(proj, np_rows, ns_rows, ns_rows, gains8, cos_s, sin_s, 1)
    lam_params = jnp.stack([_pad_lanes(lambda_q1[layer]), _pad_lanes(lambda_k1[layer]),
                            _pad_lanes(lambda_q2[layer]), _pad_lanes(lambda_k2[layer])]
                           + [jnp.zeros((LANES,), F32)] * 4)
    subln = attn_subln[layer].reshape(1, LANES)
    o_at_p = flash_prompt(qk_p, proj, lam_params, subln, batch, t_pad, tq)
    n_pool = cache_k.shape[1]
    ck = cache_k[layer].reshape(n_pool, PAGE_SIZE, DA_HEADS * LANES)
    cv = cache_v[layer].reshape(n_pool, PAGE_SIZE, DA_HEADS * LANES)
    o_at_s = paged_decode(page_table, qk_s, proj, np_rows, ck, cv, lam_params, subln, dec_batch, dec_seq)

    o_dn = jnp.concatenate([o_dn_p, o_dn_s], axis=0)
    o_at = jnp.concatenate([o_at_p, o_at_s], axis=0)
    h1 = merge_out(x_all, o_dn, o_at, proj, w_branch_dn[layer].astype(BF16),
                   w_branch_attn[layer].astype(BF16), w_out[layer].astype(BF16), 512)

    qp = rms_matmul(h1, norm_ffn[layer], w_peer_q[layer].astype(BF16), 1024, 512)
    sk = peer_sub_keys[layer]
    half = sk.shape[-1]
    keys2 = jnp.stack([jnp.pad(sk[:, 0], ((0, 0), (0, 0), (0, half))),
                       jnp.pad(sk[:, 1], ((0, 0), (0, 0), (half, 0)))], axis=1)
    keys2 = keys2.reshape(2 * PEER_HEADS, PEER_NKEYS, 2 * half).astype(BF16)
    eidx_t, gate_t = peer_topk(qp, keys2, 256)
    uv = jnp.concatenate([expert_u[layer], expert_v[layer]], axis=1)
    h2 = peer_mix(eidx_t.T, gate_t.T, h1, norm_ffn[layer], uv, SUBLANES)

    hd = DA_HEADS
    y_prompt = h2[:np_rows].reshape(batch, t_pad, d)[:, N_META:t_real]
    y_sample = h2[np_rows:].reshape(dec_batch, dec_seq, d)
    k_prompt = qk_p[:, hd * LANES:].reshape(batch, t_pad, hd, LANES)[:, :t_real][None]
    v_prompt = proj[:np_rows, COL_AV:COL_AV + hd * LANES].reshape(batch, t_pad, hd, LANES)[:, :t_real][None]
    conv_prompt = proj[:np_rows, :DN_CONV_CH].reshape(batch, t_pad, DN_CONV_CH)[:, t_real - (CONV_W - 1):t_real][None]
    k_sample = qk_s[:, hd * LANES:].reshape(dec_batch, dec_seq, hd, LANES)[None]
    v_sample = proj[np_rows:, COL_AV:COL_AV + hd * LANES].reshape(dec_batch, dec_seq, hd, LANES)[None]
    conv_sample = proj[np_rows:, :DN_CONV_CH].reshape(dec_batch, dec_seq, DN_CONV_CH)[:, dec_seq - (CONV_W - 1):][None]
    return (y_prompt, y_sample, k_prompt, v_prompt, conv_prompt, s_p[None],
            k_sample, v_sample, conv_sample, s_s[None])
```

```python
import functools
import math

import jax
import jax.numpy as jnp
from jax import lax
from jax.experimental import pallas as pl
from jax.experimental.pallas import tpu as pltpu

F32 = jnp.float32
BF16 = jnp.bfloat16
HI = lax.Precision.HIGHEST

N_META = 16
DN_HEADS = 8
DN_DK = 64
DN_QK = 512
DN_CONV_CH = 1536
CONV_W = 4
DN_CHUNK = 64
DA_HEADS = 4
DA_DH = 64
ROPE_THETA = 10000.0
RMS_EPS = 1e-6
L2_EPS = 1e-6
PEER_NKEYS = 128
PEER_HEADS = 8
PEER_TOPK = 16
PAGE_SIZE = 128
PAGES_PER_STEP = 4
LAM_INIT = 0.8 - 0.6 * math.exp(-0.3 * 0)

LANES = 128
SUBLANES = 8
VMEM_LIMIT = 48 * 1024 * 1024

COL_CONV = 0
COL_DZ = 1536
COL_AQ = 2048
COL_AK = 2560
COL_AV = 3072
COL_GA = 3584
COL_GB = 4608
COL_DBDA = 5632
PROJ_PAD = 5760


def _nt(a, b, precision=None):
    return lax.dot_general(a, b, (((1,), (1,)), ((), ())), precision=precision,
                           preferred_element_type=F32)


def _tn(a, b, precision=None):
    return lax.dot_general(a, b, (((0,), (0,)), ((), ())), precision=precision,
                           preferred_element_type=F32)


def _mm(a, b, precision=None):
    return jnp.dot(a, b, precision=precision, preferred_element_type=F32)


def _b(x):
    return x.astype(BF16)


def _mm3(a, b):
    ah, bh = _b(a), _b(b)
    al, bl = _b(a - ah.astype(F32)), _b(b - bh.astype(F32))
    return _mm(ah, bh) + (_mm(ah, bl) + _mm(al, bh))


def _params(sem):
    return pltpu.CompilerParams(dimension_semantics=sem, vmem_limit_bytes=VMEM_LIMIT)


def _rms_matmul_kernel(x_ref, g_ref, w_ref, o_ref, xn_ref):
    @pl.when(pl.program_id(1) == 0)
    def _():
        x = x_ref[...]
        ms = jnp.mean(x * x, axis=-1, keepdims=True)
        xn_ref[...] = (x * lax.rsqrt(ms + RMS_EPS) * g_ref[...]).astype(BF16)

    o_ref[...] = _mm(xn_ref[...], w_ref[...])


def rms_matmul(x, gain, w_bf16, tm, tn):
    n, d = x.shape
    nout = w_bf16.shape[1]
    return pl.pallas_call(
        _rms_matmul_kernel,
        grid=(n // tm, nout // tn),
        in_specs=[pl.BlockSpec((tm, d), lambda i, j: (i, 0)),
                  pl.BlockSpec((1, d), lambda i, j: (0, 0)),
                  pl.BlockSpec((d, tn), lambda i, j: (0, j))],
        out_specs=pl.BlockSpec((tm, tn), lambda i, j: (i, j)),
        out_shape=jax.ShapeDtypeStruct((n, nout), F32),
        scratch_shapes=[pltpu.VMEM((tm, d), BF16)],
        compiler_params=_params(("parallel", "arbitrary")),
        name="rms_matmul",
    )(x, gain.reshape(1, d), w_bf16)


def _softplus(x):
    return jnp.maximum(x, 0.0) + jnp.log1p(jnp.exp(-jnp.abs(x)))


def _delta_kernel(cin_ref, dz_ref, dbda_ref, cs_ref, s0_ref, cw_ref, hp_ref, gain_ref,
                  o_ref, sout_ref, ext_ref, s_ref, *, chunk, t_real, n_chunks):
    c = pl.program_id(1)
    C = chunk

    @pl.when(c == 0)
    def _():
        ext_ref[0:SUBLANES, :] = cs_ref[0]
        s_ref[...] = s0_ref[0]

    x = cin_ref[...]
    ext_ref[SUBLANES:SUBLANES + C, :] = x
    conv = x * cw_ref[3:4, :]
    for j in range(CONV_W - 1):
        off = SUBLANES - (CONV_W - 1) + j
        conv = conv + ext_ref[off:off + C, :] * cw_ref[j:j + 1, :]
    ext_ref[0:SUBLANES, :] = ext_ref[C:C + SUBLANES, :]
    conv = conv * jax.nn.sigmoid(conv)

    row = lax.broadcasted_iota(jnp.int32, (C, C), 0)
    col = lax.broadcasted_iota(jnp.int32, (C, C), 1)
    tril = row >= col
    strict = row > col
    eye = (row == col).astype(F32)

    valid = (c * C + lax.broadcasted_iota(jnp.int32, (C, LANES), 0)) < t_real
    dbda = dbda_ref[...]
    beta_all = jnp.where(valid, jax.nn.sigmoid(dbda), 0.0)
    g_all = jnp.where(valid, -jnp.exp(hp_ref[0:1, :]) * _softplus(dbda + hp_ref[1:2, :]), 0.0)
    gc_all = _mm(tril.astype(F32), g_all, HI)
    gc_t = _tn(g_all, (row <= col).astype(F32), HI)

    heads = range(DN_HEADS)
    cut = lambda base, h: conv[:, base + h * DN_DK: base + (h + 1) * DN_DK]
    l2n = lambda t: t * lax.rsqrt(jnp.sum(t * t, axis=-1, keepdims=True) + L2_EPS)
    q = [l2n(cut(0, h)) * (DN_DK ** -0.5) for h in heads]
    k = [l2n(cut(DN_QK, h)) for h in heads]
    v = [cut(2 * DN_QK, h) for h in heads]
    beta = [beta_all[:, h:h + 1] for h in heads]
    gcol = [gc_all[:, DN_HEADS + h:DN_HEADS + h + 1] for h in heads]
    grow = [gc_t[DN_HEADS + h:DN_HEADS + h + 1, :] for h in heads]
    glast = [gc_all[C - 1:C, DN_HEADS + h:DN_HEADS + h + 1] for h in heads]
    gamma = [jnp.where(tril, jnp.exp(jnp.where(tril, gcol[h] - grow[h], 0.0)), 0.0) for h in heads]
    kb = [k[h] * beta[h] for h in heads]
    a = [jnp.where(strict, _nt(_b(kb[h]), _b(k[h])) * gamma[h], 0.0) for h in heads]
    t_inv = [eye - a[h] for h in heads]
    pw = a
    for _ in range(int(math.log2(C)) - 1):
        pw = [_mm3(pw[h], pw[h]) for h in heads]
        t_inv = [t_inv[h] + _mm3(t_inv[h], pw[h]) for h in heads]
    t_b = [_b(t_inv[h]) for h in heads]
    egc = [jnp.exp(gcol[h]) for h in heads]
    u = [_mm(t_b[h], _b(v[h] * beta[h])) for h in heads]
    w = [_mm(t_b[h], _b(kb[h] * egc[h])) for h in heads]
    qk = [_nt(_b(q[h]), _b(k[h])) * gamma[h] for h in heads]
    s = [s_ref[h] for h in heads]
    s_b = [_b(s[h]) for h in heads]
    v_new = [u[h] - _mm(_b(w[h]), s_b[h]) for h in heads]
    o = [_mm(_b(q[h] * egc[h]), s_b[h]) + _mm(_b(qk[h]), _b(v_new[h])) for h in heads]
    for h in heads:
        kd = k[h] * jnp.exp(glast[h] - gcol[h])
        s_ref[h] = s[h] * jnp.exp(glast[h]) + _tn(_b(kd), _b(v_new[h]))
    outs = [o[h] * lax.rsqrt(jnp.mean(o[h] * o[h], axis=-1, keepdims=True) + RMS_EPS) for h in heads]
    o_all = jnp.concatenate(outs, axis=-1)
    dz = dz_ref[...]
    o_ref[...] = o_all * gain_ref[...] * (dz * jax.nn.sigmoid(dz))

    @pl.when(c == n_chunks - 1)
    def _():
        sout_ref[0] = s_ref[...]


def delta_branch(proj, row_off, batch, t_pad, t_real, chunk, conv_state8, s0, conv_w8, hp, gain512):
    n_chunks = t_pad // chunk
    blk_off = row_off // chunk
    kern = functools.partial(_delta_kernel, chunk=chunk, t_real=t_real, n_chunks=n_chunks)
    rowmap = lambda col: (lambda b, c: (blk_off + b * n_chunks + c, col))
    return pl.pallas_call(
        kern,
        grid=(batch, n_chunks),
        in_specs=[pl.BlockSpec((chunk, DN_CONV_CH), rowmap(COL_CONV // DN_CONV_CH)),
                  pl.BlockSpec((chunk, DN_QK), rowmap(COL_DZ // DN_QK)),
                  pl.BlockSpec((chunk, LANES), rowmap(COL_DBDA // LANES)),
                  pl.BlockSpec((1, SUBLANES, DN_CONV_CH), lambda b, c: (b, 0, 0)),
                  pl.BlockSpec((1, DN_HEADS, DN_DK, DN_DK), lambda b, c: (b, 0, 0, 0)),
                  pl.BlockSpec((SUBLANES, DN_CONV_CH), lambda b, c: (0, 0)),
                  pl.BlockSpec((SUBLANES, LANES), lambda b, c: (0, 0)),
                  pl.BlockSpec((1, DN_QK), lambda b, c: (0, 0))],
        out_specs=[pl.BlockSpec((chunk, DN_QK), lambda b, c: (b * n_chunks + c, 0)),
                   pl.BlockSpec((1, DN_HEADS, DN_DK, DN_DK), lambda b, c: (b, 0, 0, 0))],
        out_shape=[jax.ShapeDtypeStruct((batch * t_pad, DN_QK), F32),
                   jax.ShapeDtypeStruct((batch, DN_HEADS, DN_DK, DN_DK), F32)],
        scratch_shapes=[pltpu.VMEM((chunk + SUBLANES, DN_CONV_CH), F32),
                        pltpu.VMEM((DN_HEADS, DN_DK, DN_DK), F32)],
        compiler_params=_params(("parallel", "arbitrary")),
        name=f"delta_c{chunk}",
    )(proj, proj, proj, conv_state8, s0, conv_w8, hp, gain512)


def _qk_prep_kernel(x_ref, g_ref, cos_ref, sin_ref, o_ref):
    x = x_ref[...]
    lane = lax.broadcasted_iota(jnp.int32, x.shape, 1)
    lo = lane < DA_DH
    x2 = x * x
    s_lo = jnp.sum(jnp.where(lo, x2, 0.0), axis=-1, keepdims=True)
    s_hi = jnp.sum(jnp.where(lo, 0.0, x2), axis=-1, keepdims=True)
    ms = jnp.where(lo, s_lo, s_hi) * (1.0 / DA_DH)
    y = x * lax.rsqrt(ms + RMS_EPS) * g_ref[0]
    first = (lane % DA_DH) < (DA_DH // 2)
    rot = jnp.where(first, -pltpu.roll(y, LANES - DA_DH // 2, 1), pltpu.roll(y, DA_DH // 2, 1))
    o_ref[...] = y * cos_ref[...] + rot * sin_ref[...]


def qk_prep(proj, row_off, n_rows, tm, gains8, cos, sin, tiles_per_seq):
    blk_off = row_off // tm
    tab_map = lambda i, j: (i % tiles_per_seq, 0)
    return pl.pallas_call(
        _qk_prep_kernel,
        grid=(n_rows // tm, 2 * DA_HEADS),
        in_specs=[pl.BlockSpec((tm, LANES), lambda i, j: (blk_off + i, COL_AQ // LANES + j)),
                  pl.BlockSpec((1, 1, LANES), lambda i, j: (j, 0, 0)),
                  pl.BlockSpec((tm, LANES), tab_map),
                  pl.BlockSpec((tm, LANES), tab_map)],
        out_specs=pl.BlockSpec((tm, LANES), lambda i, j: (i, j)),
        out_shape=jax.ShapeDtypeStruct((n_rows, 2 * DA_HEADS * LANES), F32),
        compiler_params=_params(("parallel", "parallel")),
        name="qk_prep",
    )(proj, gains8, cos, sin)


def _rope_tables(pos):
    half = DA_DH // 2
    inv = ROPE_THETA ** (-jnp.arange(half, dtype=F32) / half)
    ang = pos.astype(F32)[:, None] * inv[None, :]
    return jnp.tile(jnp.cos(ang), (1, LANES // half)), jnp.tile(jnp.sin(ang), (1, LANES // half))


def _lam(lp_ref):
    lp = lp_ref[...]
    s1 = jnp.sum(lp[0:1, :] * lp[1:2, :], axis=-1, keepdims=True)
    s2 = jnp.sum(lp[2:3, :] * lp[3:4, :], axis=-1, keepdims=True)
    return jnp.exp(s1) - jnp.exp(s2) + LAM_INIT


def _softmax_step(q, k, v, mask, m_prev, l_prev, acc_prev):
    s = _nt(q, k)
    if mask is not None:
        s = jnp.where(mask, s, -jnp.inf)
    m_new = jnp.maximum(m_prev, jnp.max(s, axis=-1, keepdims=True))
    p = jnp.exp(s - m_new)
    corr = jnp.exp(m_prev - m_new)
    l_new = l_prev * corr + jnp.sum(p, axis=-1, keepdims=True)
    acc_new = acc_prev * corr + _mm(p.astype(BF16), v)
    return m_new, l_new, acc_new


def _diff_out(lam, l1, a1, l2, a2, gain):
    o = a1 / l1 - lam * (a2 / l2)
    return o * lax.rsqrt(jnp.mean(o * o, axis=-1, keepdims=True) + RMS_EPS) * gain * (1.0 - LAM_INIT)


def _split_q(q):
    lane = lax.broadcasted_iota(jnp.int32, q.shape, 1)
    return (jnp.where(lane < DA_DH, q, 0.0).astype(BF16),
            jnp.where(lane < DA_DH, 0.0, q).astype(BF16))


def _flash_kernel(lp_ref, q_ref, k_ref, v_ref, gain_ref, o_ref, m_ref, l_ref, acc_ref):
    qi = pl.program_id(2)
    ki = pl.program_id(3)

    @pl.when(ki == 0)
    def _():
        m_ref[...] = jnp.full(m_ref.shape, jnp.finfo(F32).min, F32)
        l_ref[...] = jnp.zeros(l_ref.shape, F32)
        acc_ref[...] = jnp.zeros(acc_ref.shape, F32)

    def update(mask):
        qs = _split_q(q_ref[...])
        k = k_ref[...].astype(BF16)
        v = v_ref[...].astype(BF16)
        for m in range(2):
            m_new, l_new, acc_new = _softmax_step(qs[m], k, v, mask, m_ref[m], l_ref[m], acc_ref[m])
            m_ref[m] = m_new
            l_ref[m] = l_new
            acc_ref[m] = acc_new

    @pl.when(ki < qi)
    def _():
        update(None)

    @pl.when(ki == qi)
    def _():
        tq, tk = q_ref.shape[0], k_ref.shape[0]
        row = lax.broadcasted_iota(jnp.int32, (tq, tk), 0)
        col = lax.broadcasted_iota(jnp.int32, (tq, tk), 1)
        update(col <= row)
        o_ref[...] = _diff_out(_lam(lp_ref), l_ref[0], acc_ref[0], l_ref[1], acc_ref[1], gain_ref[...])


def flash_prompt(qk, proj, lam_params, subln_gain, batch, t_pad, tq):
    nq = t_pad // tq
    return pl.pallas_call(
        _flash_kernel,
        grid=(batch, DA_HEADS, nq, nq),
        in_specs=[pl.BlockSpec((SUBLANES, LANES), lambda b, h, qi, ki: (0, 0)),
                  pl.BlockSpec((tq, LANES), lambda b, h, qi, ki: (b * nq + qi, h)),
                  pl.BlockSpec((tq, LANES), lambda b, h, qi, ki: (b * nq + jnp.minimum(ki, qi), DA_HEADS + h)),
                  pl.BlockSpec((tq, LANES), lambda b, h, qi, ki: (b * nq + jnp.minimum(ki, qi), COL_AV // LANES + h)),
                  pl.BlockSpec((1, LANES), lambda b, h, qi, ki: (0, 0))],
        out_specs=pl.BlockSpec((tq, LANES), lambda b, h, qi, ki: (b * nq + qi, h)),
        out_shape=jax.ShapeDtypeStruct((batch * t_pad, DA_HEADS * LANES), F32),
        scratch_shapes=[pltpu.VMEM((2, tq, 1), F32), pltpu.VMEM((2, tq, 1), F32),
                        pltpu.VMEM((2, tq, LANES), F32)],
        compiler_params=_params(("parallel", "parallel", "parallel", "arbitrary")),
        name="flash_prompt",
    )(lam_params, qk, qk, proj, subln_gain)


def _paged_kernel(pt_ref, lp_ref, q_ref, kn_ref, vn_ref, *rest, n_steps, pps):
    kp_refs, vp_refs = rest[:pps], rest[pps:2 * pps]
    gain_ref, o_ref, wq_ref, m_ref, l_ref, acc_ref = rest[2 * pps:]
    p = pl.program_id(1)
    S = q_ref.shape[0]
    R = 2 * DA_HEADS * S

    @pl.when(p == 0)
    def _():
        qt = jnp.concatenate([q_ref[...]] * (2 * DA_HEADS), axis=0)
        r = lax.broadcasted_iota(jnp.int32, qt.shape, 0)
        lane = lax.broadcasted_iota(jnp.int32, qt.shape, 1)
        keep = (lane // LANES == r // (2 * S)) & ((lane % LANES) // DA_DH == (r // S) % 2)
        wq_ref[...] = jnp.where(keep, qt, 0.0).astype(BF16)
        m_ref[...] = jnp.full(m_ref.shape, jnp.finfo(F32).min, F32)
        l_ref[...] = jnp.zeros(l_ref.shape, F32)
        acc_ref[...] = jnp.zeros(acc_ref.shape, F32)

    def update(k, v, mask):
        m_new, l_new, acc_new = _softmax_step(wq_ref[...], k, v, mask, m_ref[...], l_ref[...], acc_ref[...])
        m_ref[...] = m_new
        l_ref[...] = l_new
        acc_ref[...] = acc_new

    @pl.when(p < n_steps)
    def _():
        k = jnp.concatenate([_b(kp[...]) for kp in kp_refs], axis=0)
        v = jnp.concatenate([_b(vp[...]) for vp in vp_refs], axis=0)
        update(k, v, None)

    @pl.when(p == n_steps)
    def _():
        row = lax.broadcasted_iota(jnp.int32, (R, S), 0)
        col = lax.broadcasted_iota(jnp.int32, (R, S), 1)
        update(_b(kn_ref[...]), _b(vn_ref[...]), col <= row % S)
        lam = _lam(lp_ref)
        outs = []
        for h in range(DA_HEADS):
            r1 = slice(2 * S * h, 2 * S * h + S)
            r2 = slice(2 * S * h + S, 2 * S * (h + 1))
            cs = slice(h * LANES, (h + 1) * LANES)
            outs.append(_diff_out(lam, l_ref[r1, :], acc_ref[r1, cs], l_ref[r2, :], acc_ref[r2, cs], gain_ref[...]))
        o_ref[...] = jnp.concatenate(outs, axis=-1)


def paged_decode(page_table, qk_s, proj, row_off, cache_k, cache_v, page_off, lam_params, subln_gain,
                 dec_batch, dec_seq, pps):
    n_pages = page_table.shape[1]
    n_steps = n_pages // pps
    width = DA_HEADS * LANES
    rows = 2 * DA_HEADS * dec_seq
    blk_off = row_off // dec_seq

    def page_map(j):
        return lambda b, p, pt: (page_off + pt[b, jnp.minimum(p * pps + j, n_pages - 1)], 0, 0)

    page_specs = [pl.BlockSpec((None, PAGE_SIZE, width), page_map(j)) for j in range(pps)]
    grid_spec = pltpu.PrefetchScalarGridSpec(
        num_scalar_prefetch=1,
        grid=(dec_batch, n_steps + 1),
        in_specs=[pl.BlockSpec((SUBLANES, LANES), lambda b, p, pt: (0, 0)),
                  pl.BlockSpec((dec_seq, width), lambda b, p, pt: (b, 0)),
                  pl.BlockSpec((dec_seq, width), lambda b, p, pt: (b, 1)),
                  pl.BlockSpec((dec_seq, width), lambda b, p, pt: (blk_off + b, COL_AV // width))]
                 + page_specs + page_specs
                 + [pl.BlockSpec((1, LANES), lambda b, p, pt: (0, 0))],
        out_specs=pl.BlockSpec((dec_seq, width), lambda b, p, pt: (b, 0)),
        scratch_shapes=[pltpu.VMEM((rows, width), BF16),
                        pltpu.VMEM((rows, 1), F32),
                        pltpu.VMEM((rows, 1), F32),
                        pltpu.VMEM((rows, width), F32)])
    return pl.pallas_call(
        functools.partial(_paged_kernel, n_steps=n_steps, pps=pps),
        grid_spec=grid_spec,
        out_shape=jax.ShapeDtypeStruct((dec_batch * dec_seq, width), F32),
        compiler_params=_params(("parallel", "arbitrary")),
        name="paged_decode",
    )(page_table, lam_params, qk_s, qk_s, proj, *([cache_k] * pps), *([cache_v] * pps), subln_gain)


def _merge_kernel(x_ref, odn_ref, oat_ref, ga0_ref, ga1_ref, gb0_ref, gb1_ref,
                  wdn_ref, wat_ref, wout_ref, o_ref):
    half = ga0_ref.shape[1]
    bd = _mm(odn_ref[...].astype(BF16), wdn_ref[...])
    ba = _mm(oat_ref[...].astype(BF16), wat_ref[...])
    m0 = jax.nn.sigmoid(ga0_ref[...]) * bd[:, :half] + jax.nn.sigmoid(gb0_ref[...]) * ba[:, :half]
    m1 = jax.nn.sigmoid(ga1_ref[...]) * bd[:, half:] + jax.nn.sigmoid(gb1_ref[...]) * ba[:, half:]
    merged = jnp.concatenate([m0, m1], axis=-1).astype(BF16)
    o_ref[...] = x_ref[...] + _mm(merged, wout_ref[...])


def merge_out(x, o_dn, o_attn, proj, w_dn, w_at, w_out, tm):
    n, d = x.shape
    half = d // 2
    col = lambda c: (lambda i: (i, c))
    const = lambda i: (0, 0)
    return pl.pallas_call(
        _merge_kernel,
        grid=(n // tm,),
        in_specs=[pl.BlockSpec((tm, d), col(0)),
                  pl.BlockSpec((tm, o_dn.shape[1]), col(0)),
                  pl.BlockSpec((tm, o_attn.shape[1]), col(0)),
                  pl.BlockSpec((tm, half), col(COL_GA // half)),
                  pl.BlockSpec((tm, half), col(COL_GA // half + 1)),
                  pl.BlockSpec((tm, half), col(COL_GB // half)),
                  pl.BlockSpec((tm, half), col(COL_GB // half + 1)),
                  pl.BlockSpec(w_dn.shape, const),
                  pl.BlockSpec(w_at.shape, const),
                  pl.BlockSpec(w_out.shape, const)],
        out_specs=pl.BlockSpec((tm, d), col(0)),
        out_shape=jax.ShapeDtypeStruct((n, d), F32),
        compiler_params=_params(("parallel",)),
        name="merge_out",
    )(x, o_dn, o_attn, proj, proj, proj, proj, w_dn, w_at, w_out)


def _take_top(s, pos, k, payload=None):
    vals, picks = [], []
    sentinel = float(s.shape[0])
    for _ in range(k):
        m = jnp.max(s, axis=0, keepdims=True)
        first = jnp.min(jnp.where(s == m, pos, sentinel), axis=0, keepdims=True)
        hit = pos == first
        vals.append(m)
        if payload is None:
            picks.append(first)
        else:
            picks.append(jnp.max(jnp.where(hit, payload, -1.0), axis=0, keepdims=True))
        s = jnp.where(hit, -jnp.inf, s)
    return jnp.concatenate(vals, axis=0), jnp.concatenate(picks, axis=0)


def _peer_topk_kernel(q_ref, keys_ref, eidx_ref, gate_ref):
    tt = q_ref.shape[0]
    kpos = lax.broadcasted_iota(jnp.int32, (PEER_NKEYS, tt), 0).astype(F32)
    cpos = lax.broadcasted_iota(jnp.int32, (PEER_TOPK * PEER_TOPK, tt), 0).astype(F32)
    for h in range(PEER_HEADS):
        qh = q_ref[:, h * LANES:(h + 1) * LANES].astype(BF16)
        tops = []
        for c in range(2):
            s = _nt(keys_ref[2 * h + c], qh)
            tops.append(_take_top(s, kpos, PEER_TOPK))
        (v0, i0), (v1, i1) = tops
        cand = jnp.concatenate([v0[a:a + 1, :] + v1 for a in range(PEER_TOPK)], axis=0)
        cidx = jnp.concatenate([i0[a:a + 1, :] * float(PEER_NKEYS) + i1 for a in range(PEER_TOPK)], axis=0)
        best, eidx = _take_top(cand, cpos, PEER_TOPK, payload=cidx)
        e = jnp.exp(best - best[0:1, :])
        gate = e / jnp.sum(e, axis=0, keepdims=True)
        eidx_ref[h * PEER_TOPK:(h + 1) * PEER_TOPK, :] = eidx.astype(jnp.int32)
        gate_ref[h * PEER_TOPK:(h + 1) * PEER_TOPK, :] = gate


def peer_topk(qp, keys2, tt):
    n = qp.shape[0]
    rows = PEER_HEADS * PEER_TOPK
    return pl.pallas_call(
        _peer_topk_kernel,
        grid=(n // tt,),
        in_specs=[pl.BlockSpec((tt, qp.shape[1]), lambda i: (i, 0)),
                  pl.BlockSpec(keys2.shape, lambda i: (0, 0, 0))],
        out_specs=[pl.BlockSpec((rows, tt), lambda i: (0, i)),
                   pl.BlockSpec((rows, tt), lambda i: (0, i))],
        out_shape=[jax.ShapeDtypeStruct((rows, n), jnp.int32),
                   jax.ShapeDtypeStruct((rows, n), F32)],
        compiler_params=_params(("parallel",)),
        name="peer_topk",
    )(qp, keys2)


def _gelu(x):
    return 0.5 * x * (1.0 + lax.erf(x * (2.0 ** -0.5)))


def _peer_mix_kernel(idx_cur_ref, idx_nxt_ref, gate_ref, h_ref, g_ref, tab_ref, o_ref, buf_ref, sem_ref,
                     *, tt, n_sel, n_steps):
    i = pl.program_id(0)
    rows = tt * n_sel

    def row_copy(e, slot, r):
        return pltpu.make_async_copy(tab_ref.at[e, pl.ds(0, 1)], buf_ref.at[slot, pl.ds(r, 1)], sem_ref.at[slot])

    def wait_slot(slot):
        pltpu.make_async_copy(buf_ref.at[slot], buf_ref.at[slot], sem_ref.at[slot]).wait()

    half = n_sel // 2

    def mix_and_issue(slot, first_row, idx_ref, idx_row, next_slot):
        def issue(t, k0):
            for k in range(k0, k0 + half):
                row_copy(idx_ref[idx_row + t, k], next_slot, t * n_sel + k).start()

        hb = h_ref[first_row:first_row + tt, :]
        xn = hb * lax.rsqrt(jnp.mean(hb * hb, axis=-1, keepdims=True) + RMS_EPS) * g_ref[...]
        er = lax.broadcasted_iota(jnp.int32, (n_sel, n_sel), 0)
        ec = lax.broadcasted_iota(jnp.int32, (n_sel, n_sel), 1)
        gate_t = _nt((er == ec).astype(F32), gate_ref[first_row:first_row + tt, :], HI)
        hi_mask = jnp.uint32(0xFFFF0000)
        cols = []
        for t in range(tt):
            issue(t, 0)
            word = buf_ref[slot, t * n_sel:(t + 1) * n_sel, :]
            u = lax.bitcast_convert_type(word & hi_mask, F32)
            cols.append(jnp.sum(u * xn[t:t + 1, :], axis=-1, keepdims=True))
        w = gate_t * _gelu(jnp.concatenate(cols, axis=1))
        ys = []
        for t in range(tt):
            issue(t, half)
            word = buf_ref[slot, t * n_sel:(t + 1) * n_sel, :]
            v = lax.bitcast_convert_type(word << 16, F32)
            ys.append(jnp.sum(v * w[:, t:t + 1], axis=0, keepdims=True))
        o_ref[first_row:first_row + tt, :] = hb + jnp.concatenate(ys, axis=0)

    @pl.when(i == 0)
    def _():
        def body(r, carry):
            row_copy(idx_cur_ref[r // n_sel, r % n_sel], 0, r).start()
            return carry
        lax.fori_loop(0, rows, body, 0)

    wait_slot(0)
    mix_and_issue(0, 0, idx_cur_ref, tt, 1)
    wait_slot(1)
    mix_and_issue(1, tt, idx_nxt_ref, 0, 0)

    @pl.when(i == n_steps - 1)
    def _():
        wait_slot(0)


def peer_mix(eidx, gate, h, gain, table, tt):
    n, d = h.shape
    n_sel = eidx.shape[1]
    n_steps = n // (2 * tt)
    kern = functools.partial(_peer_mix_kernel, tt=tt, n_sel=n_sel, n_steps=n_steps)
    return pl.pallas_call(
        kern,
        grid=(n_steps,),
        in_specs=[pl.BlockSpec((2 * tt, n_sel), lambda i: (i, 0), memory_space=pltpu.SMEM),
                  pl.BlockSpec((2 * tt, n_sel), lambda i: (jnp.minimum(i + 1, n_steps - 1), 0),
                               memory_space=pltpu.SMEM),
                  pl.BlockSpec((2 * tt, n_sel), lambda i: (i, 0)),
                  pl.BlockSpec((2 * tt, d), lambda i: (i, 0)),
                  pl.BlockSpec((1, d), lambda i: (0, 0)),
                  pl.BlockSpec(memory_space=pl.ANY)],
        out_specs=pl.BlockSpec((2 * tt, d), lambda i: (i, 0)),
        out_shape=jax.ShapeDtypeStruct((n, d), F32),
        scratch_shapes=[pltpu.VMEM((2, tt * n_sel, d), jnp.uint32), pltpu.SemaphoreType.DMA((2,))],
        compiler_params=_params(("arbitrary",)),
        name="peer_mix",
    )(eidx, eidx, gate, h, gain.reshape(1, d), table)


def pack_expert_table(u, v):
    ub = lax.bitcast_convert_type(u.astype(BF16), jnp.uint16).astype(jnp.uint32)
    vb = lax.bitcast_convert_type(v.astype(BF16), jnp.uint16).astype(jnp.uint32)
    packed = (ub << 16) | vb
    return jnp.pad(packed[:, None, :], ((0, 0), (0, SUBLANES - 1), (0, 0)))


def _pad_lanes(v, width=LANES):
    return jnp.pad(v, (0, width - v.shape[0]))


def kernel(x_prompt, x_sample, cache_k, cache_v, page_table, state_conv, state_delta, meta_tokens, norm_mix, w_in, conv_w, a_log, dt_bias, dn_norm, q_norm, k_norm, lambda_q1, lambda_k1, lambda_q2, lambda_k2, attn_subln, w_branch_dn, w_branch_attn, w_out, norm_ffn, w_peer_q, peer_sub_keys, expert_u, expert_v):
    batch, seq, d = x_prompt.shape
    dec_batch, dec_seq, _ = x_sample.shape
    layer = 0
    t_real = N_META + seq
    tq = 3 * LANES
    t_pad = -(-t_real // tq) * tq
    np_rows = batch * t_pad
    ns_rows = dec_batch * dec_seq
    n = np_rows + ns_rows

    meta = jnp.broadcast_to(meta_tokens[None], (batch, N_META, d))
    hp = jnp.concatenate([meta, x_prompt, jnp.zeros((batch, t_pad - t_real, d), F32)], axis=1)
    x_all = jnp.concatenate([hp.reshape(np_rows, d), x_sample.reshape(ns_rows, d)], axis=0)

    wi = w_in[layer]
    pts = [0, 512, 1024, 1536, 2048, 2056, 2064, 2576, 3088, 3600, 4624, 5648]
    seg = [wi[:, pts[i]:pts[i + 1]] for i in range(11)]
    dq, dk, dv, dz, db, da, aq, ak, av, ga, gb = seg
    w_in_r = jnp.concatenate([dq, dk, dv, dz, aq, ak, av, ga, gb, db, da,
                              jnp.zeros((d, PROJ_PAD - 5648), F32)], axis=1).astype(BF16)

    proj = rms_matmul(x_all, norm_mix[layer], w_in_r, 1024, 640)

    conv_w8 = jnp.pad(conv_w[layer], ((0, SUBLANES - CONV_W), (0, 0)))
    hp_rows = jnp.zeros((SUBLANES, LANES), F32)
    hp_rows = hp_rows.at[0, DN_HEADS:2 * DN_HEADS].set(a_log[layer])
    hp_rows = hp_rows.at[1, DN_HEADS:2 * DN_HEADS].set(dt_bias[layer])
    gain512 = jnp.tile(dn_norm[layer], DN_HEADS).reshape(1, DN_QK)
    cs_p = jnp.zeros((batch, SUBLANES, DN_CONV_CH), F32)
    s0_p = jnp.zeros((batch, DN_HEADS, DN_DK, DN_DK), F32)
    o_dn_p, s_p = delta_branch(proj, 0, batch, t_pad, t_real, DN_CHUNK, cs_p, s0_p, conv_w8, hp_rows, gain512)
    cs_s = jnp.pad(state_conv[layer], ((0, 0), (SUBLANES - (CONV_W - 1), 0), (0, 0)))
    o_dn_s, s_s = delta_branch(proj, np_rows, dec_batch, dec_seq, dec_seq, dec_seq, cs_s,
                               state_delta[layer], conv_w8, hp_rows, gain512)

    gq = jnp.tile(q_norm[layer], 2) * (DA_DH ** -0.5)
    gk = jnp.tile(k_norm[layer], 2)
    gains8 = jnp.concatenate([jnp.tile(gq[None], (DA_HEADS, 1)), jnp.tile(gk[None], (DA_HEADS, 1))], axis=0)
    gains8 = gains8.reshape(2 * DA_HEADS, 1, LANES)
    cos_p, sin_p = _rope_tables(jnp.arange(t_pad))
    past_len = page_table.shape[1] * PAGE_SIZE
    cos_s, sin_s = _rope_tables(past_len + (jnp.arange(ns_rows) % dec_seq))
    qk_p = qk_prep(proj, 0, np_rows, tq, gains8, cos_p, sin_p, t_pad // tq)
    qk_s = qk_prep(proj, np_rows, ns_rows, ns_rows, gains8, cos_s, sin_s, 1)
    lam_params = jnp.stack([_pad_lanes(lambda_q1[layer]), _pad_lanes(lambda_k1[layer]),
                            _pad_lanes(lambda_q2[layer]), _pad_lanes(lambda_k2[layer])]
                           + [jnp.zeros((LANES,), F32)] * 4)
    subln = attn_subln[layer].reshape(1, LANES)
    o_at_p = flash_prompt(qk_p, proj, lam_params, subln, batch, t_pad, tq)
    depth, n_pool = cache_k.shape[:2]
    ck = cache_k.reshape(depth * n_pool, PAGE_SIZE, DA_HEADS * LANES)
    cv = cache_v.reshape(depth * n_pool, PAGE_SIZE, DA_HEADS * LANES)
    o_at_s = paged_decode(page_table, qk_s, proj, np_rows, ck, cv, layer * n_pool, lam_params, subln,
                          dec_batch, dec_seq, math.gcd(page_table.shape[1], PAGES_PER_STEP))

    o_dn = jnp.concatenate([o_dn_p, o_dn_s], axis=0)
    o_at = jnp.concatenate([o_at_p, o_at_s], axis=0)
    h1 = merge_out(x_all, o_dn, o_at, proj, w_branch_dn[layer].astype(BF16),
                   w_branch_attn[layer].astype(BF16), w_out[layer].astype(BF16), 512)

    qp = rms_matmul(h1, norm_ffn[layer], w_peer_q[layer].astype(BF16), 1024, 512)
    sk = peer_sub_keys[layer]
    half = sk.shape[-1]
    keys2 = jnp.stack([jnp.pad(sk[:, 0], ((0, 0), (0, 0), (0, half))),
                       jnp.pad(sk[:, 1], ((0, 0), (0, 0), (half, 0)))], axis=1)
    keys2 = keys2.reshape(2 * PEER_HEADS, PEER_NKEYS, 2 * half).astype(BF16)
    eidx_t, gate_t = peer_topk(qp, keys2, 256)
    table = pack_expert_table(expert_u[layer], expert_v[layer])
    h2 = peer_mix(eidx_t.T, gate_t.T, h1, norm_ffn[layer], table, SUBLANES)

    hd = DA_HEADS
    y_prompt = h2[:np_rows].reshape(batch, t_pad, d)[:, N_META:t_real]
    y_sample = h2[np_rows:].reshape(dec_batch, dec_seq, d)
    k_prompt = qk_p[:, hd * LANES:].reshape(batch, t_pad, hd, LANES)[:, :t_real][None]
    v_prompt = proj[:np_rows, COL_AV:COL_AV + hd * LANES].reshape(batch, t_pad, hd, LANES)[:, :t_real][None]
    conv_prompt = proj[:np_rows, :DN_CONV_CH].reshape(batch, t_pad, DN_CONV_CH)[:, t_real - (CONV_W - 1):t_real][None]
    k_sample = qk_s[:, hd * LANES:].reshape(dec_batch, dec_seq, hd, LANES)[None]
    v_sample = proj[np_rows:, COL_AV:COL_AV + hd * LANES].reshape(dec_batch, dec_seq, hd, LANES)[None]
    conv_sample = proj[np_rows:, :DN_CONV_CH].reshape(dec_batch, dec_seq, DN_CONV_CH)[:, dec_seq - (CONV_W - 1):][None]
    return (y_prompt, y_sample, k_prompt, v_prompt, conv_prompt, s_p[None],
            k_sample, v_sample, conv_sample, s_s[None])
```

```python
import functools
import math

import jax
import jax.numpy as jnp
from jax import lax
from jax.experimental import pallas as pl
from jax.experimental.pallas import tpu as pltpu

F32 = jnp.float32
BF16 = jnp.bfloat16
HI = lax.Precision.HIGHEST

N_META = 16
DN_HEADS = 8
DN_DK = 64
DN_QK = 512
DN_CONV_CH = 1536
CONV_W = 4
DN_CHUNK = 64
DA_HEADS = 4
DA_DH = 64
ROPE_THETA = 10000.0
RMS_EPS = 1e-6
L2_EPS = 1e-6
PEER_NKEYS = 128
PEER_HEADS = 8
PEER_TOPK = 16
PAGE_SIZE = 128
PAGES_PER_STEP = 4
LAM_INIT = 0.8 - 0.6 * math.exp(-0.3 * 0)

LANES = 128
SUBLANES = 8
VMEM_LIMIT = 48 * 1024 * 1024

COL_CONV = 0
COL_DZ = 1536
COL_AQ = 2048
COL_AK = 2560
COL_AV = 3072
COL_GA = 3584
COL_GB = 4608
COL_DBDA = 5632
PROJ_PAD = 5760


def _nt(a, b, precision=None):
    return lax.dot_general(a, b, (((1,), (1,)), ((), ())), precision=precision,
                           preferred_element_type=F32)


def _tn(a, b, precision=None):
    return lax.dot_general(a, b, (((0,), (0,)), ((), ())), precision=precision,
                           preferred_element_type=F32)


def _mm(a, b, precision=None):
    return jnp.dot(a, b, precision=precision, preferred_element_type=F32)


def _b(x):
    return x.astype(BF16)


def _mm3(a, b):
    ah, bh = _b(a), _b(b)
    al, bl = _b(a - ah.astype(F32)), _b(b - bh.astype(F32))
    return _mm(ah, bh) + (_mm(ah, bl) + _mm(al, bh))


def _params(sem):
    return pltpu.CompilerParams(dimension_semantics=sem, vmem_limit_bytes=VMEM_LIMIT)


def _rms_matmul_kernel(x_ref, g_ref, w_ref, o_ref, xn_ref):
    @pl.when(pl.program_id(1) == 0)
    def _():
        x = x_ref[...]
        ms = jnp.mean(x * x, axis=-1, keepdims=True)
        xn_ref[...] = (x * lax.rsqrt(ms + RMS_EPS) * g_ref[...]).astype(BF16)

    o_ref[...] = _mm(xn_ref[...], w_ref[...])


def rms_matmul(x, gain, w_bf16, tm, tn):
    n, d = x.shape
    nout = w_bf16.shape[1]
    return pl.pallas_call(
        _rms_matmul_kernel,
        grid=(n // tm, nout // tn),
        in_specs=[pl.BlockSpec((tm, d), lambda i, j: (i, 0)),
                  pl.BlockSpec((1, d), lambda i, j: (0, 0)),
                  pl.BlockSpec((d, tn), lambda i, j: (0, j))],
        out_specs=pl.BlockSpec((tm, tn), lambda i, j: (i, j)),
        out_shape=jax.ShapeDtypeStruct((n, nout), F32),
        scratch_shapes=[pltpu.VMEM((tm, d), BF16)],
        compiler_params=_params(("parallel", "arbitrary")),
        name="rms_matmul",
    )(x, gain.reshape(1, d), w_bf16)


def _softplus(x):
    return jnp.maximum(x, 0.0) + jnp.log1p(jnp.exp(-jnp.abs(x)))


def _delta_kernel(cin_ref, dz_ref, dbda_ref, cs_ref, s0_ref, cw_ref, hp_ref, gain_ref,
                  o_ref, sout_ref, ext_ref, s_ref, *, chunk, t_real, n_chunks):
    c = pl.program_id(1)
    C = chunk

    @pl.when(c == 0)
    def _():
        ext_ref[0:SUBLANES, :] = cs_ref[0]
        s_ref[...] = s0_ref[0]

    x = cin_ref[...]
    ext_ref[SUBLANES:SUBLANES + C, :] = x
    conv = x * cw_ref[3:4, :]
    for j in range(CONV_W - 1):
        off = SUBLANES - (CONV_W - 1) + j
        conv = conv + ext_ref[off:off + C, :] * cw_ref[j:j + 1, :]
    ext_ref[0:SUBLANES, :] = ext_ref[C:C + SUBLANES, :]
    conv = conv * jax.nn.sigmoid(conv)

    row = lax.broadcasted_iota(jnp.int32, (C, C), 0)
    col = lax.broadcasted_iota(jnp.int32, (C, C), 1)
    tril = row >= col
    strict = row > col
    eye = (row == col).astype(F32)

    valid = (c * C + lax.broadcasted_iota(jnp.int32, (C, LANES), 0)) < t_real
    dbda = dbda_ref[...]
    beta_all = jnp.where(valid, jax.nn.sigmoid(dbda), 0.0)
    g_all = jnp.where(valid, -jnp.exp(hp_ref[0:1, :]) * _softplus(dbda + hp_ref[1:2, :]), 0.0)
    gc_all = _mm(tril.astype(F32), g_all, HI)
    gc_t = _tn(g_all, (row <= col).astype(F32), HI)

    heads = range(DN_HEADS)
    cut = lambda base, h: conv[:, base + h * DN_DK: base + (h + 1) * DN_DK]
    l2n = lambda t: t * lax.rsqrt(jnp.sum(t * t, axis=-1, keepdims=True) + L2_EPS)
    q = [l2n(cut(0, h)) * (DN_DK ** -0.5) for h in heads]
    k = [l2n(cut(DN_QK, h)) for h in heads]
    v = [cut(2 * DN_QK, h) for h in heads]
    beta = [beta_all[:, h:h + 1] for h in heads]
    gcol = [gc_all[:, DN_HEADS + h:DN_HEADS + h + 1] for h in heads]
    grow = [gc_t[DN_HEADS + h:DN_HEADS + h + 1, :] for h in heads]
    glast = [gc_all[C - 1:C, DN_HEADS + h:DN_HEADS + h + 1] for h in heads]
    gamma = [jnp.where(tril, jnp.exp(jnp.where(tril, gcol[h] - grow[h], 0.0)), 0.0) for h in heads]
    kb = [k[h] * beta[h] for h in heads]
    a = [jnp.where(strict, _nt(_b(kb[h]), _b(k[h])) * gamma[h], 0.0) for h in heads]
    t_inv = [eye - a[h] for h in heads]
    pw = a
    for _ in range(int(math.log2(C)) - 1):
        pw = [_mm3(pw[h], pw[h]) for h in heads]
        t_inv = [t_inv[h] + _mm3(t_inv[h], pw[h]) for h in heads]
    t_b = [_b(t_inv[h]) for h in heads]
    egc = [jnp.exp(gcol[h]) for h in heads]
    u = [_mm(t_b[h], _b(v[h] * beta[h])) for h in heads]
    w = [_mm(t_b[h], _b(kb[h] * egc[h])) for h in heads]
    qk = [_nt(_b(q[h]), _b(k[h])) * gamma[h] for h in heads]
    s = [s_ref[h] for h in heads]
    s_b = [_b(s[h]) for h in heads]
    v_new = [u[h] - _mm(_b(w[h]), s_b[h]) for h in heads]
    o = [_mm(_b(q[h] * egc[h]), s_b[h]) + _mm(_b(qk[h]), _b(v_new[h])) for h in heads]
    for h in heads:
        kd = k[h] * jnp.exp(glast[h] - gcol[h])
        s_ref[h] = s[h] * jnp.exp(glast[h]) + _tn(_b(kd), _b(v_new[h]))
    outs = [o[h] * lax.rsqrt(jnp.mean(o[h] * o[h], axis=-1, keepdims=True) + RMS_EPS) for h in heads]
    o_all = jnp.concatenate(outs, axis=-1)
    dz = dz_ref[...]
    o_ref[...] = o_all * gain_ref[...] * (dz * jax.nn.sigmoid(dz))

    @pl.when(c == n_chunks - 1)
    def _():
        sout_ref[0] = s_ref[...]


def delta_branch(proj, row_off, batch, t_pad, t_real, chunk, conv_state8, s0, conv_w8, hp, gain512):
    n_chunks = t_pad // chunk
    blk_off = row_off // chunk
    kern = functools.partial(_delta_kernel, chunk=chunk, t_real=t_real, n_chunks=n_chunks)
    rowmap = lambda col: (lambda b, c: (blk_off + b * n_chunks + c, col))
    return pl.pallas_call(
        kern,
        grid=(batch, n_chunks),
        in_specs=[pl.BlockSpec((chunk, DN_CONV_CH), rowmap(COL_CONV // DN_CONV_CH)),
                  pl.BlockSpec((chunk, DN_QK), rowmap(COL_DZ // DN_QK)),
                  pl.BlockSpec((chunk, LANES), rowmap(COL_DBDA // LANES)),
                  pl.BlockSpec((1, SUBLANES, DN_CONV_CH), lambda b, c: (b, 0, 0)),
                  pl.BlockSpec((1, DN_HEADS, DN_DK, DN_DK), lambda b, c: (b, 0, 0, 0)),
                  pl.BlockSpec((SUBLANES, DN_CONV_CH), lambda b, c: (0, 0)),
                  pl.BlockSpec((SUBLANES, LANES), lambda b, c: (0, 0)),
                  pl.BlockSpec((1, DN_QK), lambda b, c: (0, 0))],
        out_specs=[pl.BlockSpec((chunk, DN_QK), lambda b, c: (b * n_chunks + c, 0)),
                   pl.BlockSpec((1, DN_HEADS, DN_DK, DN_DK), lambda b, c: (b, 0, 0, 0))],
        out_shape=[jax.ShapeDtypeStruct((batch * t_pad, DN_QK), F32),
                   jax.ShapeDtypeStruct((batch, DN_HEADS, DN_DK, DN_DK), F32)],
        scratch_shapes=[pltpu.VMEM((chunk + SUBLANES, DN_CONV_CH), F32),
                        pltpu.VMEM((DN_HEADS, DN_DK, DN_DK), F32)],
        compiler_params=_params(("parallel", "arbitrary")),
        name=f"delta_c{chunk}",
    )(proj, proj, proj, conv_state8, s0, conv_w8, hp, gain512)


def _qk_prep_kernel(x_ref, g_ref, cos_ref, sin_ref, o_ref):
    x = x_ref[...]
    lane = lax.broadcasted_iota(jnp.int32, x.shape, 1)
    lo = lane < DA_DH
    x2 = x * x
    s_lo = jnp.sum(jnp.where(lo, x2, 0.0), axis=-1, keepdims=True)
    s_hi = jnp.sum(jnp.where(lo, 0.0, x2), axis=-1, keepdims=True)
    ms = jnp.where(lo, s_lo, s_hi) * (1.0 / DA_DH)
    y = x * lax.rsqrt(ms + RMS_EPS) * g_ref[0]
    first = (lane % DA_DH) < (DA_DH // 2)
    rot = jnp.where(first, -pltpu.roll(y, LANES - DA_DH // 2, 1), pltpu.roll(y, DA_DH // 2, 1))
    o_ref[...] = y * cos_ref[...] + rot * sin_ref[...]


def qk_prep(proj, row_off, n_rows, tm, gains8, cos, sin, tiles_per_seq):
    blk_off = row_off // tm
    tab_map = lambda i, j: (i % tiles_per_seq, 0)
    return pl.pallas_call(
        _qk_prep_kernel,
        grid=(n_rows // tm, 2 * DA_HEADS),
        in_specs=[pl.BlockSpec((tm, LANES), lambda i, j: (blk_off + i, COL_AQ // LANES + j)),
                  pl.BlockSpec((1, 1, LANES), lambda i, j: (j, 0, 0)),
                  pl.BlockSpec((tm, LANES), tab_map),
                  pl.BlockSpec((tm, LANES), tab_map)],
        out_specs=pl.BlockSpec((tm, LANES), lambda i, j: (i, j)),
        out_shape=jax.ShapeDtypeStruct((n_rows, 2 * DA_HEADS * LANES), F32),
        compiler_params=_params(("parallel", "parallel")),
        name="qk_prep",
    )(proj, gains8, cos, sin)


def _rope_tables(pos):
    half = DA_DH // 2
    inv = ROPE_THETA ** (-jnp.arange(half, dtype=F32) / half)
    ang = pos.astype(F32)[:, None] * inv[None, :]
    return jnp.tile(jnp.cos(ang), (1, LANES // half)), jnp.tile(jnp.sin(ang), (1, LANES // half))


def _lam(lp_ref):
    lp = lp_ref[...]
    s1 = jnp.sum(lp[0:1, :] * lp[1:2, :], axis=-1, keepdims=True)
    s2 = jnp.sum(lp[2:3, :] * lp[3:4, :], axis=-1, keepdims=True)
    return jnp.exp(s1) - jnp.exp(s2) + LAM_INIT


def _softmax_step(q, k, v, mask, m_prev, l_prev, acc_prev):
    s = _nt(q, k)
    if mask is not None:
        s = jnp.where(mask, s, -jnp.inf)
    m_new = jnp.maximum(m_prev, jnp.max(s, axis=-1, keepdims=True))
    p = jnp.exp(s - m_new)
    corr = jnp.exp(m_prev - m_new)
    l_new = l_prev * corr + jnp.sum(p, axis=-1, keepdims=True)
    acc_new = acc_prev * corr + _mm(p.astype(BF16), v)
    return m_new, l_new, acc_new


def _diff_out(lam, l1, a1, l2, a2, gain):
    o = a1 / l1 - lam * (a2 / l2)
    return o * lax.rsqrt(jnp.mean(o * o, axis=-1, keepdims=True) + RMS_EPS) * gain * (1.0 - LAM_INIT)


def _split_q(q):
    lane = lax.broadcasted_iota(jnp.int32, q.shape, 1)
    return (jnp.where(lane < DA_DH, q, 0.0).astype(BF16),
            jnp.where(lane < DA_DH, 0.0, q).astype(BF16))


def _flash_kernel(lp_ref, q_ref, k_ref, v_ref, gain_ref, o_ref, m_ref, l_ref, acc_ref):
    qi = pl.program_id(2)
    ki = pl.program_id(3)

    @pl.when(ki == 0)
    def _():
        m_ref[...] = jnp.full(m_ref.shape, jnp.finfo(F32).min, F32)
        l_ref[...] = jnp.zeros(l_ref.shape, F32)
        acc_ref[...] = jnp.zeros(acc_ref.shape, F32)

    def update(mask):
        qs = _split_q(q_ref[...])
        k = k_ref[...].astype(BF16)
        v = v_ref[...].astype(BF16)
        for m in range(2):
            m_new, l_new, acc_new = _softmax_step(qs[m], k, v, mask, m_ref[m], l_ref[m], acc_ref[m])
            m_ref[m] = m_new
            l_ref[m] = l_new
            acc_ref[m] = acc_new

    @pl.when(ki < qi)
    def _():
        update(None)

    @pl.when(ki == qi)
    def _():
        tq, tk = q_ref.shape[0], k_ref.shape[0]
        row = lax.broadcasted_iota(jnp.int32, (tq, tk), 0)
        col = lax.broadcasted_iota(jnp.int32, (tq, tk), 1)
        update(col <= row)
        o_ref[...] = _diff_out(_lam(lp_ref), l_ref[0], acc_ref[0], l_ref[1], acc_ref[1], gain_ref[...])


def flash_prompt(qk, proj, lam_params, subln_gain, batch, t_pad, tq):
    nq = t_pad // tq
    return pl.pallas_call(
        _flash_kernel,
        grid=(batch, DA_HEADS, nq, nq),
        in_specs=[pl.BlockSpec((SUBLANES, LANES), lambda b, h, qi, ki: (0, 0)),
                  pl.BlockSpec((tq, LANES), lambda b, h, qi, ki: (b * nq + qi, h)),
                  pl.BlockSpec((tq, LANES), lambda b, h, qi, ki: (b * nq + jnp.minimum(ki, qi), DA_HEADS + h)),
                  pl.BlockSpec((tq, LANES), lambda b, h, qi, ki: (b * nq + jnp.minimum(ki, qi), COL_AV // LANES + h)),
                  pl.BlockSpec((1, LANES), lambda b, h, qi, ki: (0, 0))],
        out_specs=pl.BlockSpec((tq, LANES), lambda b, h, qi, ki: (b * nq + qi, h)),
        out_shape=jax.ShapeDtypeStruct((batch * t_pad, DA_HEADS * LANES), F32),
        scratch_shapes=[pltpu.VMEM((2, tq, 1), F32), pltpu.VMEM((2, tq, 1), F32),
                        pltpu.VMEM((2, tq, LANES), F32)],
        compiler_params=_params(("parallel", "parallel", "parallel", "arbitrary")),
        name="flash_prompt",
    )(lam_params, qk, qk, proj, subln_gain)


def _paged_kernel(pt_ref, lp_ref, q_ref, kn_ref, vn_ref, *rest, n_steps, pps):
    kp_refs, vp_refs = rest[:pps], rest[pps:2 * pps]
    gain_ref, o_ref, wq_ref, m_ref, l_ref, acc_ref = rest[2 * pps:]
    p = pl.program_id(1)
    S = q_ref.shape[0]
    R = 2 * DA_HEADS * S

    heads = range(DA_HEADS)
    hrows = lambda h: slice(2 * S * h, 2 * S * (h + 1))

    @pl.when(p == 0)
    def _():
        blocks = []
        for h in heads:
            qh = q_ref[:, h * LANES:(h + 1) * LANES]
            qq = jnp.concatenate([qh, qh], axis=0)
            r = lax.broadcasted_iota(jnp.int32, qq.shape, 0)
            lane = lax.broadcasted_iota(jnp.int32, qq.shape, 1)
            blocks.append(jnp.where(lane // DA_DH == r // S, qq, 0.0))
        wq_ref[...] = jnp.concatenate(blocks, axis=0).astype(BF16)
        m_ref[...] = jnp.full(m_ref.shape, jnp.finfo(F32).min, F32)
        l_ref[...] = jnp.zeros(l_ref.shape, F32)
        acc_ref[...] = jnp.zeros(acc_ref.shape, F32)

    def update(ks, vs, mask):
        wq = wq_ref[...]
        s = jnp.concatenate([_nt(wq[hrows(h), :], ks[h]) for h in heads], axis=0)
        if mask is not None:
            s = jnp.where(mask, s, -jnp.inf)
        m_prev = m_ref[...]
        m_new = jnp.maximum(m_prev, jnp.max(s, axis=-1, keepdims=True))
        pr = jnp.exp(s - m_new)
        corr = jnp.exp(m_prev - m_new)
        l_ref[...] = l_ref[...] * corr + jnp.sum(pr, axis=-1, keepdims=True)
        pv = jnp.concatenate([_mm(_b(pr[hrows(h), :]), vs[h]) for h in heads], axis=0)
        acc_ref[...] = acc_ref[...] * corr + pv
        m_ref[...] = m_new

    @pl.when(p < n_steps)
    def _():
        ks = [jnp.concatenate([_b(kp[:, h, :]) for kp in kp_refs], axis=0) for h in heads]
        vs = [jnp.concatenate([_b(vp[:, h, :]) for vp in vp_refs], axis=0) for h in heads]
        update(ks, vs, None)

    @pl.when(p == n_steps)
    def _():
        row = lax.broadcasted_iota(jnp.int32, (R, S), 0)
        col = lax.broadcasted_iota(jnp.int32, (R, S), 1)
        ks = [_b(kn_ref[:, h * LANES:(h + 1) * LANES]) for h in heads]
        vs = [_b(vn_ref[:, h * LANES:(h + 1) * LANES]) for h in heads]
        update(ks, vs, col <= row % S)
        lam = _lam(lp_ref)
        outs = []
        for h in heads:
            r1 = slice(2 * S * h, 2 * S * h + S)
            r2 = slice(2 * S * h + S, 2 * S * (h + 1))
            outs.append(_diff_out(lam, l_ref[r1, :], acc_ref[r1, :], l_ref[r2, :], acc_ref[r2, :], gain_ref[...]))
        o_ref[...] = jnp.concatenate(outs, axis=-1)


def paged_decode(page_table, qk_s, proj, row_off, cache_k, cache_v, layer, lam_params, subln_gain,
                 dec_batch, dec_seq, pps):
    n_pages = page_table.shape[1]
    n_steps = n_pages // pps
    width = DA_HEADS * LANES
    rows = 2 * DA_HEADS * dec_seq
    blk_off = row_off // dec_seq

    def page_map(j):
        return lambda b, p, pt: (layer, pt[b, jnp.minimum(p * pps + j, n_pages - 1)], 0, 0, 0)

    page_specs = [pl.BlockSpec((None, None, PAGE_SIZE, DA_HEADS, LANES), page_map(j)) for j in range(pps)]
    grid_spec = pltpu.PrefetchScalarGridSpec(
        num_scalar_prefetch=1,
        grid=(dec_batch, n_steps + 1),
        in_specs=[pl.BlockSpec((SUBLANES, LANES), lambda b, p, pt: (0, 0)),
                  pl.BlockSpec((dec_seq, width), lambda b, p, pt: (b, 0)),
                  pl.BlockSpec((dec_seq, width), lambda b, p, pt: (b, 1)),
                  pl.BlockSpec((dec_seq, width), lambda b, p, pt: (blk_off + b, COL_AV // width))]
                 + page_specs + page_specs
                 + [pl.BlockSpec((1, LANES), lambda b, p, pt: (0, 0))],
        out_specs=pl.BlockSpec((dec_seq, width), lambda b, p, pt: (b, 0)),
        scratch_shapes=[pltpu.VMEM((rows, LANES), BF16),
                        pltpu.VMEM((rows, 1), F32),
                        pltpu.VMEM((rows, 1), F32),
                        pltpu.VMEM((rows, LANES), F32)])
    return pl.pallas_call(
        functools.partial(_paged_kernel, n_steps=n_steps, pps=pps),
        grid_spec=grid_spec,
        out_shape=jax.ShapeDtypeStruct((dec_batch * dec_seq, width), F32),
        compiler_params=_params(("parallel", "arbitrary")),
        name="paged_decode",
    )(page_table, lam_params, qk_s, qk_s, proj, *([cache_k] * pps), *([cache_v] * pps), subln_gain)


def _merge_kernel(x_ref, odn_ref, oat_ref, ga0_ref, ga1_ref, gb0_ref, gb1_ref,
                  wdn_ref, wat_ref, wout_ref, o_ref):
    half = ga0_ref.shape[1]
    bd = _mm(odn_ref[...].astype(BF16), wdn_ref[...])
    ba = _mm(oat_ref[...].astype(BF16), wat_ref[...])
    m0 = jax.nn.sigmoid(ga0_ref[...]) * bd[:, :half] + jax.nn.sigmoid(gb0_ref[...]) * ba[:, :half]
    m1 = jax.nn.sigmoid(ga1_ref[...]) * bd[:, half:] + jax.nn.sigmoid(gb1_ref[...]) * ba[:, half:]
    merged = jnp.concatenate([m0, m1], axis=-1).astype(BF16)
    o_ref[...] = x_ref[...] + _mm(merged, wout_ref[...])


def merge_out(x, o_dn, o_attn, proj, w_dn, w_at, w_out, tm):
    n, d = x.shape
    half = d // 2
    col = lambda c: (lambda i: (i, c))
    const = lambda i: (0, 0)
    return pl.pallas_call(
        _merge_kernel,
        grid=(n // tm,),
        in_specs=[pl.BlockSpec((tm, d), col(0)),
                  pl.BlockSpec((tm, o_dn.shape[1]), col(0)),
                  pl.BlockSpec((tm, o_attn.shape[1]), col(0)),
                  pl.BlockSpec((tm, half), col(COL_GA // half)),
                  pl.BlockSpec((tm, half), col(COL_GA // half + 1)),
                  pl.BlockSpec((tm, half), col(COL_GB // half)),
                  pl.BlockSpec((tm, half), col(COL_GB // half + 1)),
                  pl.BlockSpec(w_dn.shape, const),
                  pl.BlockSpec(w_at.shape, const),
                  pl.BlockSpec(w_out.shape, const)],
        out_specs=pl.BlockSpec((tm, d), col(0)),
        out_shape=jax.ShapeDtypeStruct((n, d), F32),
        compiler_params=_params(("parallel",)),
        name="merge_out",
    )(x, o_dn, o_attn, proj, proj, proj, proj, w_dn, w_at, w_out)


def _take_top(s, pos, k, payload=None):
    vals, picks = [], []
    sentinel = float(s.shape[0])
    for _ in range(k):
        m = jnp.max(s, axis=0, keepdims=True)
        first = jnp.min(jnp.where(s == m, pos, sentinel), axis=0, keepdims=True)
        hit = pos == first
        vals.append(m)
        if payload is None:
            picks.append(first)
        else:
            picks.append(jnp.max(jnp.where(hit, payload, -1.0), axis=0, keepdims=True))
        s = jnp.where(hit, -jnp.inf, s)
    return jnp.concatenate(vals, axis=0), jnp.concatenate(picks, axis=0)


def _peer_topk_kernel(q_ref, keys_ref, eidx_ref, gate_ref):
    tt = q_ref.shape[0]
    kpos = lax.broadcasted_iota(jnp.int32, (PEER_NKEYS, tt), 0).astype(F32)
    cpos = lax.broadcasted_iota(jnp.int32, (PEER_TOPK * PEER_TOPK, tt), 0).astype(F32)
    for h in range(PEER_HEADS):
        qh = q_ref[:, h * LANES:(h + 1) * LANES].astype(BF16)
        tops = []
        for c in range(2):
            s = _nt(keys_ref[2 * h + c], qh)
            tops.append(_take_top(s, kpos, PEER_TOPK))
        (v0, i0), (v1, i1) = tops
        cand = jnp.concatenate([v0[a:a + 1, :] + v1 for a in range(PEER_TOPK)], axis=0)
        cidx = jnp.concatenate([i0[a:a + 1, :] * float(PEER_NKEYS) + i1 for a in range(PEER_TOPK)], axis=0)
        best, eidx = _take_top(cand, cpos, PEER_TOPK, payload=cidx)
        e = jnp.exp(best - best[0:1, :])
        gate = e / jnp.sum(e, axis=0, keepdims=True)
        eidx_ref[h * PEER_TOPK:(h + 1) * PEER_TOPK, :] = eidx.astype(jnp.int32)
        gate_ref[h * PEER_TOPK:(h + 1) * PEER_TOPK, :] = gate


def peer_topk(qp, keys2, tt):
    n = qp.shape[0]
    rows = PEER_HEADS * PEER_TOPK
    return pl.pallas_call(
        _peer_topk_kernel,
        grid=(n // tt,),
        in_specs=[pl.BlockSpec((tt, qp.shape[1]), lambda i: (i, 0)),
                  pl.BlockSpec(keys2.shape, lambda i: (0, 0, 0))],
        out_specs=[pl.BlockSpec((rows, tt), lambda i: (0, i)),
                   pl.BlockSpec((rows, tt), lambda i: (0, i))],
        out_shape=[jax.ShapeDtypeStruct((rows, n), jnp.int32),
                   jax.ShapeDtypeStruct((rows, n), F32)],
        compiler_params=_params(("parallel",)),
        name="peer_topk",
    )(qp, keys2)


def _gelu(x):
    return 0.5 * x * (1.0 + lax.erf(x * (2.0 ** -0.5)))


def _peer_mix_kernel(idx_cur_ref, idx_nxt_ref, gate_ref, h_ref, g_ref, tab_ref, o_ref, buf_ref, sem_ref,
                     *, tt, n_sel, n_steps):
    i = pl.program_id(0)
    rows = tt * n_sel

    def row_copy(e, slot, r):
        return pltpu.make_async_copy(tab_ref.at[e, pl.ds(0, 1)], buf_ref.at[slot, pl.ds(r, 1)], sem_ref.at[slot])

    def wait_slot(slot):
        pltpu.make_async_copy(buf_ref.at[slot], buf_ref.at[slot], sem_ref.at[slot]).wait()

    half = n_sel // 2

    def mix_and_issue(slot, first_row, idx_ref, idx_row, next_slot):
        def issue(t, k0):
            for k in range(k0, k0 + half):
                row_copy(idx_ref[idx_row + t, k], next_slot, t * n_sel + k).start()

        hb = h_ref[first_row:first_row + tt, :]
        xn = hb * lax.rsqrt(jnp.mean(hb * hb, axis=-1, keepdims=True) + RMS_EPS) * g_ref[...]
        er = lax.broadcasted_iota(jnp.int32, (n_sel, n_sel), 0)
        ec = lax.broadcasted_iota(jnp.int32, (n_sel, n_sel), 1)
        gate_t = _nt((er == ec).astype(F32), gate_ref[first_row:first_row + tt, :], HI)
        hi_mask = jnp.uint32(0xFFFF0000)
        cols = []
        for t in range(tt):
            issue(t, 0)
            word = buf_ref[slot, t * n_sel:(t + 1) * n_sel, :]
            u = lax.bitcast_convert_type(word & hi_mask, F32)
            cols.append(jnp.sum(u * xn[t:t + 1, :], axis=-1, keepdims=True))
        w = gate_t * _gelu(jnp.concatenate(cols, axis=1))
        ys = []
        for t in range(tt):
            issue(t, half)
            word = buf_ref[slot, t * n_sel:(t + 1) * n_sel, :]
            v = lax.bitcast_convert_type(word << 16, F32)
            ys.append(jnp.sum(v * w[:, t:t + 1], axis=0, keepdims=True))
        o_ref[first_row:first_row + tt, :] = hb + jnp.concatenate(ys, axis=0)

    @pl.when(i == 0)
    def _():
        def body(r, carry):
            row_copy(idx_cur_ref[r // n_sel, r % n_sel], 0, r).start()
            return carry
        lax.fori_loop(0, rows, body, 0)

    wait_slot(0)
    mix_and_issue(0, 0, idx_cur_ref, tt, 1)
    wait_slot(1)
    mix_and_issue(1, tt, idx_nxt_ref, 0, 0)

    @pl.when(i == n_steps - 1)
    def _():
        wait_slot(0)


def peer_mix(eidx, gate, h, gain, table, tt):
    n, d = h.shape
    n_sel = eidx.shape[1]
    n_steps = n // (2 * tt)
    kern = functools.partial(_peer_mix_kernel, tt=tt, n_sel=n_sel, n_steps=n_steps)
    return pl.pallas_call(
        kern,
        grid=(n_steps,),
        in_specs=[pl.BlockSpec((2 * tt, n_sel), lambda i: (i, 0), memory_space=pltpu.SMEM),
                  pl.BlockSpec((2 * tt, n_sel), lambda i: (jnp.minimum(i + 1, n_steps - 1), 0),
                               memory_space=pltpu.SMEM),
                  pl.BlockSpec((2 * tt, n_sel), lambda i: (i, 0)),
                  pl.BlockSpec((2 * tt, d), lambda i: (i, 0)),
                  pl.BlockSpec((1, d), lambda i: (0, 0)),
                  pl.BlockSpec(memory_space=pl.ANY)],
        out_specs=pl.BlockSpec((2 * tt, d), lambda i: (i, 0)),
        out_shape=jax.ShapeDtypeStruct((n, d), F32),
        scratch_shapes=[pltpu.VMEM((2, tt * n_sel, d), jnp.uint32), pltpu.SemaphoreType.DMA((2,))],
        compiler_params=_params(("arbitrary",)),
        name="peer_mix",
    )(eidx, eidx, gate, h, gain.reshape(1, d), table)


def pack_expert_table(u, v):
    ub = lax.bitcast_convert_type(u.astype(BF16), jnp.uint16).astype(jnp.uint32)
    vb = lax.bitcast_convert_type(v.astype(BF16), jnp.uint16).astype(jnp.uint32)
    packed = (ub << 16) | vb
    return jnp.pad(packed[:, None, :], ((0, 0), (0, SUBLANES - 1), (0, 0)))


def _pad_lanes(v, width=LANES):
    return jnp.pad(v, (0, width - v.shape[0]))


def kernel(x_prompt, x_sample, cache_k, cache_v, page_table, state_conv, state_delta, meta_tokens, norm_mix, w_in, conv_w, a_log, dt_bias, dn_norm, q_norm, k_norm, lambda_q1, lambda_k1, lambda_q2, lambda_k2, attn_subln, w_branch_dn, w_branch_attn, w_out, norm_ffn, w_peer_q, peer_sub_keys, expert_u, expert_v):
    batch, seq, d = x_prompt.shape
    dec_batch, dec_seq, _ = x_sample.shape
    layer = 0
    t_real = N_META + seq
    tq = 3 * LANES
    t_pad = -(-t_real // tq) * tq
    np_rows = batch * t_pad
    ns_rows = dec_batch * dec_seq
    n = np_rows + ns_rows

    meta = jnp.broadcast_to(meta_tokens[None], (batch, N_META, d))
    hp = jnp.concatenate([meta, x_prompt, jnp.zeros((batch, t_pad - t_real, d), F32)], axis=1)
    x_all = jnp.concatenate([hp.reshape(np_rows, d), x_sample.reshape(ns_rows, d)], axis=0)

    wi = w_in[layer]
    pts = [0, 512, 1024, 1536, 2048, 2056, 2064, 2576, 3088, 3600, 4624, 5648]
    seg = [wi[:, pts[i]:pts[i + 1]] for i in range(11)]
    dq, dk, dv, dz, db, da, aq, ak, av, ga, gb = seg
    w_in_r = jnp.concatenate([dq, dk, dv, dz, aq, ak, av, ga, gb, db, da,
                              jnp.zeros((d, PROJ_PAD - 5648), F32)], axis=1).astype(BF16)

    proj = rms_matmul(x_all, norm_mix[layer], w_in_r, 1024, 640)

    conv_w8 = jnp.pad(conv_w[layer], ((0, SUBLANES - CONV_W), (0, 0)))
    hp_rows = jnp.zeros((SUBLANES, LANES), F32)
    hp_rows = hp_rows.at[0, DN_HEADS:2 * DN_HEADS].set(a_log[layer])
    hp_rows = hp_rows.at[1, DN_HEADS:2 * DN_HEADS].set(dt_bias[layer])
    gain512 = jnp.tile(dn_norm[layer], DN_HEADS).reshape(1, DN_QK)
    cs_p = jnp.zeros((batch, SUBLANES, DN_CONV_CH), F32)
    s0_p = jnp.zeros((batch, DN_HEADS, DN_DK, DN_DK), F32)
    o_dn_p, s_p = delta_branch(proj, 0, batch, t_pad, t_real, DN_CHUNK, cs_p, s0_p, conv_w8, hp_rows, gain512)
    cs_s = jnp.pad(state_conv[layer], ((0, 0), (SUBLANES - (CONV_W - 1), 0), (0, 0)))
    o_dn_s, s_s = delta_branch(proj, np_rows, dec_batch, dec_seq, dec_seq, dec_seq, cs_s,
                               state_delta[layer], conv_w8, hp_rows, gain512)

    gq = jnp.tile(q_norm[layer], 2) * (DA_DH ** -0.5)
    gk = jnp.tile(k_norm[layer], 2)
    gains8 = jnp.concatenate([jnp.tile(gq[None], (DA_HEADS, 1)), jnp.tile(gk[None], (DA_HEADS, 1))], axis=0)
    gains8 = gains8.reshape(2 * DA_HEADS, 1, LANES)
    cos_p, sin_p = _rope_tables(jnp.arange(t_pad))
    past_len = page_table.shape[1] * PAGE_SIZE
    cos_s, sin_s = _rope_tables(past_len + (jnp.arange(ns_rows) % dec_seq))
    qk_p = qk_prep(proj, 0, np_rows, tq, gains8, cos_p, sin_p, t_pad // tq)
    qk_s = qk_prep(proj, np_rows, ns_rows, ns_rows, gains8, cos_s, sin_s, 1)
    lam_params = jnp.stack([_pad_lanes(lambda_q1[layer]), _pad_lanes(lambda_k1[layer]),
                            _pad_lanes(lambda_q2[layer]), _pad_lanes(lambda_k2[layer])]
                           + [jnp.zeros((LANES,), F32)] * 4)
    subln = attn_subln[layer].reshape(1, LANES)
    o_at_p = flash_prompt(qk_p, proj, lam_params, subln, batch, t_pad, tq)
    o_at_s = paged_decode(page_table, qk_s, proj, np_rows, cache_k, cache_v, layer, lam_params, subln,
                          dec_batch, dec_seq, math.gcd(page_table.shape[1], PAGES_PER_STEP))

    o_dn = jnp.concatenate([o_dn_p, o_dn_s], axis=0)
    o_at = jnp.concatenate([o_at_p, o_at_s], axis=0)
    h1 = merge_out(x_all, o_dn, o_at, proj, w_branch_dn[layer].astype(BF16),
                   w_branch_attn[layer].astype(BF16), w_out[layer].astype(BF16), 512)

    qp = rms_matmul(h1, norm_ffn[layer], w_peer_q[layer].astype(BF16), 1024, 512)
    sk = peer_sub_keys[layer]
    half = sk.shape[-1]
    keys2 = jnp.stack([jnp.pad(sk[:, 0], ((0, 0), (0, 0), (0, half))),
                       jnp.pad(sk[:, 1], ((0, 0), (0, 0), (half, 0)))], axis=1)
    keys2 = keys2.reshape(2 * PEER_HEADS, PEER_NKEYS, 2 * half).astype(BF16)
    eidx_t, gate_t = peer_topk(qp, keys2, 256)
    table = pack_expert_table(expert_u[layer], expert_v[layer])
    h2 = peer_mix(eidx_t.T, gate_t.T, h1, norm_ffn[layer], table, SUBLANES)

    hd = DA_HEADS
    y_prompt = h2[:np_rows].reshape(batch, t_pad, d)[:, N_META:t_real]
    y_sample = h2[np_rows:].reshape(dec_batch, dec_seq, d)
    k_prompt = qk_p[:, hd * LANES:].reshape(batch, t_pad, hd, LANES)[:, :t_real][None]
    v_prompt = proj[:np_rows, COL_AV:COL_AV + hd * LANES].reshape(batch, t_pad, hd, LANES)[:, :t_real][None]
    conv_prompt = proj[:np_rows, :DN_CONV_CH].reshape(batch, t_pad, DN_CONV_CH)[:, t_real - (CONV_W - 1):t_real][None]
    k_sample = qk_s[:, hd * LANES:].reshape(dec_batch, dec_seq, hd, LANES)[None]
    v_sample = proj[np_rows:, COL_AV:COL_AV + hd * LANES].reshape(dec_batch, dec_seq, hd, LANES)[None]
    conv_sample = proj[np_rows:, :DN_CONV_CH].reshape(dec_batch, dec_seq, DN_CONV_CH)[:, dec_seq - (CONV_W - 1):][None]
    return (y_prompt, y_sample, k_prompt, v_prompt, conv_prompt, s_p[None],
            k_sample, v_sample, conv_sample, s_s[None])
```
